```python
import jax, jax.numpy as jnp
from jax import lax
import numpy as np

D_MODEL = 1024
BATCH = 16
SEQ = 2048
DEPTH = 2

GRID_W = 64
CTX_LEN = 256
Q_BLOCK = 128
ROPE_THETA = 10000.0
EPS = 1e-6
N_EVEN = (DEPTH + 1) // 2
N_ODD = DEPTH // 2
MIX_HALF = D_MODEL // 2

A_HEAD_DIM = 64
A_Q_HEADS = MIX_HALF // A_HEAD_DIM
A_KV_HEADS = 2
A_GROUP = A_Q_HEADS // A_KV_HEADS
B_GROUPS = 8
B_WIDTH = MIX_HALF
B_GROUP_DIM = B_WIDTH // B_GROUPS
B_CHUNK = 128
C_HEADS = 8
C_NOPE = 64
C_ROPE = 32
C_V = MIX_HALF // C_HEADS
C_Q_RANK = D_MODEL // 4
C_KV_RANK = D_MODEL // 8
D_WIDTH = MIX_HALF
D_CONV = 31
FF_DIM = 4 * D_MODEL
N_MOD = 6

EV_Q = A_Q_HEADS * A_HEAD_DIM
EV_KV = A_KV_HEADS * A_HEAD_DIM
EV_IN = EV_Q + 2 * EV_KV + 2 * B_WIDTH
OD_IN = C_Q_RANK + C_KV_RANK + C_ROPE + 2 * D_WIDTH

kernel_name = "hybrid_gqa_gmlp_mla_conformer_prefix_dit"


def rms_norm(x, g):
    xf = x.astype(jnp.float32)
    y = xf * lax.rsqrt(jnp.mean(xf * xf, axis=-1, keepdims=True) + EPS)
    return (y * g.astype(jnp.float32)).astype(x.dtype)


def layer_norm(x, g, b):
    xf = x.astype(jnp.float32)
    mu = jnp.mean(xf, axis=-1, keepdims=True)
    var = jnp.mean(jnp.square(xf - mu), axis=-1, keepdims=True)
    y = (xf - mu) * lax.rsqrt(var + EPS)
    return (y * g.astype(jnp.float32) + b.astype(jnp.float32)).astype(x.dtype)


def modulate(x, g, shift, scale):
    return rms_norm(x, g) * (1 + scale) + shift


def axial_angles(length, d_rot):
    rows = length // GRID_W
    row = jnp.broadcast_to(jnp.arange(rows)[:, None], (rows, GRID_W)).reshape(-1).astype(jnp.float32)
    col = jnp.broadcast_to(jnp.arange(GRID_W)[None, :], (rows, GRID_W)).reshape(-1).astype(jnp.float32)
    d_axis = d_rot // 2
    inv = ROPE_THETA ** (-jnp.arange(0, d_axis, 2, dtype=jnp.float32) / d_axis)
    return jnp.concatenate([row[:, None] * inv, col[:, None] * inv], axis=-1)


def apply_rope(x, ang):
    d = x.shape[-1]
    xf = x.astype(jnp.float32).reshape(x.shape[:-1] + (d // 2, 2))
    cos, sin = jnp.cos(ang), jnp.sin(ang)
    x0, x1 = xf[..., 0], xf[..., 1]
    out = jnp.stack([x0 * cos - x1 * sin, x0 * sin + x1 * cos], axis=-1)
    return out.reshape(x.shape).astype(x.dtype)


def to_heads(t, n_heads, head_dim):
    b, l, _ = t.shape
    return t.reshape(b, l, n_heads, head_dim).transpose(0, 2, 1, 3)


def from_heads(o):
    b, n, l, hd = o.shape
    return o.transpose(0, 2, 1, 3).reshape(b, l, n * hd)


def block_attention(q, k, v):
    b, hk, g, lq, dk = q.shape
    scale = dk ** -0.5
    qb = jnp.moveaxis(q.reshape(b, hk, g, lq // Q_BLOCK, Q_BLOCK, dk), 3, 0)

    def one_block(qi):
        s = jnp.einsum("bhgqd,bhkd->bhgqk", qi, k, preferred_element_type=jnp.float32) * scale
        p = jax.nn.softmax(s, axis=-1)
        return jnp.einsum("bhgqk,bhkd->bhgqd", p.astype(v.dtype), v)

    o = lax.map(one_block, qb)
    return jnp.moveaxis(o, 0, 3).reshape(b, hk, g, lq, v.shape[-1])


def spatial_gating(z, norm_g, w_s, b_s):
    b, l, _ = z.shape
    u, v = jnp.split(jax.nn.gelu(z), 2, axis=-1)
    v = rms_norm(v.reshape(b, l, B_GROUPS, B_GROUP_DIM), norm_g)
    v = v.reshape(b, l // B_CHUNK, B_CHUNK, B_GROUPS, B_GROUP_DIM)
    sv = jnp.einsum("gpq,bnqgc->bnpgc", w_s, v) + b_s.T[None, None, :, :, None]
    return u * sv.reshape(b, l, B_WIDTH)


def even_mixer(h_lat, h_ctx, need_ctx, w_in, q_norm_g, k_norm_g, sgu_norm_g, sgu_w, sgu_b):
    cuts = [EV_Q, EV_Q + EV_KV, EV_Q + 2 * EV_KV]
    w_q, w_k, w_v, w_z = jnp.split(w_in, cuts, axis=1)

    def gqa_q(qp, ang):
        b, l, _ = qp.shape
        q = rms_norm(to_heads(qp, A_Q_HEADS, A_HEAD_DIM), q_norm_g)
        if ang is not None:
            q = apply_rope(q, ang)
        return q.reshape(b, A_KV_HEADS, A_GROUP, l, A_HEAD_DIM)

    def gqa_kv(kp, vp, ang):
        k = rms_norm(to_heads(kp, A_KV_HEADS, A_HEAD_DIM), k_norm_g)
        if ang is not None:
            k = apply_rope(k, ang)
        return k, to_heads(vp, A_KV_HEADS, A_HEAD_DIM)

    def merge(o):
        b, hk, g, l, d = o.shape
        return from_heads(o.reshape(b, hk * g, l, d))

    b, l, _ = h_lat.shape
    ang = axial_angles(l, A_HEAD_DIM)
    qp, kp, vp, zp = jnp.split(h_lat @ w_in, cuts, axis=-1)
    kc, vc = gqa_kv(h_ctx @ w_k, h_ctx @ w_v, None)
    kl, vl = gqa_kv(kp, vp, ang)
    o_att = block_attention(gqa_q(qp, ang), jnp.concatenate([kc, kl], axis=2),
                            jnp.concatenate([vc, vl], axis=2))
    out_lat = jnp.concatenate([merge(o_att), spatial_gating(zp, sgu_norm_g, sgu_w, sgu_b)], axis=-1)
    out_ctx = None
    if need_ctx:
        oc = block_attention(gqa_q(h_ctx @ w_q, None), kc, vc)
        out_ctx = jnp.concatenate([merge(oc), spatial_gating(h_ctx @ w_z, sgu_norm_g, sgu_w, sgu_b)],
                                  axis=-1)
    return out_lat, out_ctx


def odd_mixer(h_lat, h_ctx, need_ctx, w_in, q_norm_g, kv_norm_g, w_uq, w_ukv, conv_w, conv_b,
              ln_g, ln_b):
    cuts = [C_Q_RANK, C_Q_RANK + C_KV_RANK, C_Q_RANK + C_KV_RANK + C_ROPE]
    w_cq, w_ckv, w_kr, w_cv = jnp.split(w_in, cuts, axis=1)

    def mla_q(cq, ang):
        b, l, _ = cq.shape
        q = to_heads(rms_norm(cq, q_norm_g) @ w_uq, C_HEADS, C_NOPE + C_ROPE)
        qn, qr = jnp.split(q, [C_NOPE], axis=-1)
        if ang is not None:
            qr = apply_rope(qr, ang)
        return jnp.concatenate([qn, qr], axis=-1)[:, :, None]

    def mla_kv(ckv, kr, ang):
        b, l, _ = ckv.shape
        kv = to_heads(rms_norm(ckv, kv_norm_g) @ w_ukv, C_HEADS, C_NOPE + C_V)
        kn, v = jnp.split(kv, [C_NOPE], axis=-1)
        kr = kr[:, None]
        if ang is not None:
            kr = apply_rope(kr, ang)
        k = jnp.concatenate([kn, jnp.broadcast_to(kr, (b, C_HEADS, l, C_ROPE))], axis=-1)
        return k, v

    def conformer(z):
        a, gt = jnp.split(z, 2, axis=-1)
        y = a * jax.nn.sigmoid(gt)
        y = lax.conv_general_dilated(y, conv_w[:, None, :], window_strides=(1,),
                                     padding=[(D_CONV // 2, D_CONV // 2)],
                                     dimension_numbers=("NWC", "WIO", "NWC"),
                                     feature_group_count=D_WIDTH) + conv_b
        return jax.nn.silu(layer_norm(y, ln_g, ln_b))

    b, l, _ = h_lat.shape
    ang = axial_angles(l, C_ROPE)
    cq, ckv, kr, zc = jnp.split(h_lat @ w_in, cuts, axis=-1)
    kc, vc = mla_kv(h_ctx @ w_ckv, h_ctx @ w_kr, None)
    kl, vl = mla_kv(ckv, kr, ang)
    o_att = block_attention(mla_q(cq, ang), jnp.concatenate([kc, kl], axis=2),
                            jnp.concatenate([vc, vl], axis=2))[:, :, 0]
    out_lat = jnp.concatenate([from_heads(o_att), conformer(zc)], axis=-1)
    out_ctx = None
    if need_ctx:
        oc = block_attention(mla_q(h_ctx @ w_cq, None), kc, vc)[:, :, 0]
        out_ctx = jnp.concatenate([from_heads(oc), conformer(h_ctx @ w_cv)], axis=-1)
    return out_lat, out_ctx


def sq_relu_mlp(h, w1, w2):
    return jnp.square(jax.nn.relu(h @ w1)) @ w2


def setup_inputs(seed: int = 0) -> dict:
    key = jax.random.key(seed)
    ks = jax.random.split(key, 32)

    def nrm(k, shape, scale=1.0):
        return jax.random.normal(k, shape, jnp.float32) * scale

    def gain(k, shape):
        return 1.0 + 0.05 * jax.random.normal(k, shape, jnp.float32)

    D = D_MODEL
    return {
        "x": nrm(ks[0], (BATCH, SEQ, D)),
        "c": nrm(ks[1], (BATCH, D)),
        "ctx": nrm(ks[2], (BATCH, CTX_LEN, D)),
        "c_ctx": nrm(ks[3], (D,)),
        "ada_w": nrm(ks[4], (DEPTH, D, N_MOD * D), 0.5 * D ** -0.5),
        "ada_b": nrm(ks[5], (DEPTH, N_MOD * D), 0.02),
        "norm1_g": gain(ks[6], (DEPTH, D)),
        "norm2_g": gain(ks[7], (DEPTH, D)),
        "w_out": nrm(ks[8], (DEPTH, D, D), D ** -0.5),
        "mlp_w1": nrm(ks[9], (DEPTH, D, FF_DIM), D ** -0.5),
        "mlp_w2": nrm(ks[10], (DEPTH, FF_DIM, D), FF_DIM ** -0.5),
        "ev_w_in": nrm(ks[11], (N_EVEN, D, EV_IN), D ** -0.5),
        "ev_q_norm_g": gain(ks[12], (N_EVEN, A_HEAD_DIM)),
        "ev_k_norm_g": gain(ks[13], (N_EVEN, A_HEAD_DIM)),
        "ev_sgu_norm_g": gain(ks[14], (N_EVEN, B_GROUPS, B_GROUP_DIM)),
        "ev_sgu_w": nrm(ks[15], (N_EVEN, B_GROUPS, B_CHUNK, B_CHUNK), B_CHUNK ** -0.5),
        "ev_sgu_b": gain(ks[16], (N_EVEN, B_GROUPS, B_CHUNK)),
        "od_w_in": nrm(ks[17], (N_ODD, D, OD_IN), D ** -0.5),
        "od_q_norm_g": gain(ks[18], (N_ODD, C_Q_RANK)),
        "od_kv_norm_g": gain(ks[19], (N_ODD, C_KV_RANK)),
        "od_w_uq": nrm(ks[20], (N_ODD, C_Q_RANK, C_HEADS * (C_NOPE + C_ROPE)), C_Q_RANK ** -0.5),
        "od_w_ukv": nrm(ks[21], (N_ODD, C_KV_RANK, C_HEADS * (C_NOPE + C_V)), C_KV_RANK ** -0.5),
        "od_conv_w": nrm(ks[22], (N_ODD, D_CONV, D_WIDTH), D_CONV ** -0.5),
        "od_conv_b": nrm(ks[23], (N_ODD, D_WIDTH), 0.02),
        "od_ln_g": gain(ks[24], (N_ODD, D_WIDTH)),
        "od_ln_b": nrm(ks[25], (N_ODD, D_WIDTH), 0.02),
        "final_g": gain(ks[26], (D,)),
    }


def reference(x, c, ctx, c_ctx, ada_w, ada_b, norm1_g, norm2_g, w_out, mlp_w1, mlp_w2,
              ev_w_in, ev_q_norm_g, ev_k_norm_g, ev_sgu_norm_g, ev_sgu_w, ev_sgu_b,
              od_w_in, od_q_norm_g, od_kv_norm_g, od_w_uq, od_w_ukv, od_conv_w, od_conv_b,
              od_ln_g, od_ln_b, final_g):
    x_lat, x_ctx = x, ctx
    silu_c = jax.nn.silu(c)
    silu_cc = jax.nn.silu(c_ctx)
    for i in range(DEPTH):
        last = i == DEPTH - 1
        j = i // 2
        m = jnp.split(silu_c @ ada_w[i] + ada_b[i], N_MOD, axis=-1)
        sh1, sc1, g1, sh2, sc2, g2 = [t[:, None, :] for t in m]
        sh1c, sc1c, g1c, sh2c, sc2c, g2c = jnp.split(silu_cc @ ada_w[i] + ada_b[i], N_MOD, axis=-1)

        h_lat = modulate(x_lat, norm1_g[i], sh1, sc1)
        h_ctx = modulate(x_ctx, norm1_g[i], sh1c, sc1c)
        if i % 2 == 0:
            o_lat, o_ctx = even_mixer(h_lat, h_ctx, not last, ev_w_in[j], ev_q_norm_g[j],
                                      ev_k_norm_g[j], ev_sgu_norm_g[j], ev_sgu_w[j], ev_sgu_b[j])
        else:
            o_lat, o_ctx = odd_mixer(h_lat, h_ctx, not last, od_w_in[j], od_q_norm_g[j],
                                     od_kv_norm_g[j], od_w_uq[j], od_w_ukv[j], od_conv_w[j],
                                     od_conv_b[j], od_ln_g[j], od_ln_b[j])

        x_lat = x_lat + g1 * (o_lat @ w_out[i])
        x_lat = x_lat + g2 * sq_relu_mlp(modulate(x_lat, norm2_g[i], sh2, sc2), mlp_w1[i], mlp_w2[i])
        if not last:
            x_ctx = x_ctx + g1c * (o_ctx @ w_out[i])
            x_ctx = x_ctx + g2c * sq_relu_mlp(modulate(x_ctx, norm2_g[i], sh2c, sc2c),
                                              mlp_w1[i], mlp_w2[i])
    return rms_norm(x_lat, final_g)
```

```python
import functools

import numpy as np
import jax
import jax.numpy as jnp
from jax import lax
from jax.experimental import pallas as pl
from jax.experimental.pallas import tpu as pltpu

F32 = jnp.float32
BF16 = jnp.bfloat16

D_MODEL = 1024
GRID_W = 64
ROPE_THETA = 10000.0
EPS = 1e-6
MIX_HALF = D_MODEL // 2
N_MOD = 6

A_HEAD_DIM = 64
A_Q_HEADS = 8
A_KV_HEADS = 2
A_GROUP = A_Q_HEADS // A_KV_HEADS
B_GROUPS = 8
B_GROUP_DIM = 64
B_CHUNK = 128
C_HEADS = 8
C_NOPE = 64
C_ROPE = 32
C_V = 64
C_Q_RANK = 256
C_KV_RANK = 128
D_CONV = 31
FF_DIM = 4 * D_MODEL

LANES = 128
TM = 256
FF_CHUNK = 1024
HALO = 16
VMEM_LIMIT = 56 * 1024 * 1024


def _dot(a, b):
    return jnp.dot(a, b, preferred_element_type=F32)


def _dot_nt(a, b):
    return lax.dot_general(a, b, (((1,), (1,)), ((), ())), preferred_element_type=F32)


def _rms(x):
    return x * lax.rsqrt(jnp.mean(x * x, axis=-1, keepdims=True) + EPS)


def _modulate(x, g, shift, scale):
    return (_rms(x) * g) * (1.0 + scale) + shift


def _pair_swap(x):
    even = lax.broadcasted_iota(jnp.int32, (x.shape[0], LANES), 1) % 2 == 0
    slabs = []
    for s in range(0, x.shape[-1], LANES):
        xs = x[:, s:s + LANES]
        slabs.append(jnp.where(even, pltpu.roll(xs, LANES - 1, 1), pltpu.roll(xs, 1, 1)))
    return slabs[0] if len(slabs) == 1 else jnp.concatenate(slabs, axis=-1)


def _rope(x, cos, sin_signed):
    reps = x.shape[-1] // LANES
    if reps > 1:
        cos = jnp.concatenate([cos] * reps, axis=-1)
        sin_signed = jnp.concatenate([sin_signed] * reps, axis=-1)
    return x * cos + _pair_swap(x) * sin_signed


def _group_mean_sq(x, gmat):
    w = gmat.shape[0]
    x2 = (x * x).astype(BF16)
    parts = [_dot(x2[:, i:i + w], gmat) for i in range(0, x.shape[-1], w)]
    return parts[0] if len(parts) == 1 else jnp.concatenate(parts, axis=-1)


def _ada_kernel(c_ref, w_ref, b_ref, o_ref):
    c = c_ref[...]
    s = c * (1.0 / (1.0 + jnp.exp(-c)))
    o_ref[...] = _dot(s.astype(BF16), w_ref[...].astype(BF16)) + b_ref[...]


def _ada_call(cvec, ada_w, ada_b):
    depth, d, n = ada_w.shape
    r = cvec.shape[0]
    bn = 1536
    return pl.pallas_call(
        _ada_kernel,
        out_shape=jax.ShapeDtypeStruct((depth, r, n), F32),
        grid=(depth, n // bn),
        in_specs=[
            pl.BlockSpec((r, d), lambda i, j: (0, 0)),
            pl.BlockSpec((None, d, bn), lambda i, j: (i, 0, j)),
            pl.BlockSpec((None, 1, bn), lambda i, j: (i, 0, j)),
        ],
        out_specs=pl.BlockSpec((None, r, bn), lambda i, j: (i, 0, j)),
        compiler_params=pltpu.CompilerParams(
            dimension_semantics=("arbitrary", "arbitrary"), vmem_limit_bytes=VMEM_LIMIT),
        name="ada_mod",
    )(cvec, ada_w, ada_b.reshape(depth, 1, n))


def _even_in_kernel(x_ref, mod_ref, g1_ref, w_ref, gq_ref, gk_ref, cos_ref, sin_ref, gmat_ref,
                    gmat64_ref, sgug_ref, sguw_ref, sgub_ref, q_ref, k_ref, v_ref, mb_ref):
    x = x_ref[...]
    h = _modulate(x, g1_ref[...], mod_ref[0:1, :], mod_ref[1:2, :])
    p = _dot(h.astype(BF16), w_ref[...])
    nq = A_Q_HEADS * LANES
    q = p[:, :nq]
    k = p[:, nq:nq + LANES]
    v = p[:, nq + LANES:nq + 2 * LANES]
    z = p[:, nq + 2 * LANES:]
    gmat = gmat_ref[...]
    gmat64 = gmat64_ref[...]
    cos = cos_ref[...]
    sin = sin_ref[...]

    qn = q * lax.rsqrt(_group_mean_sq(q, gmat) + EPS) * gq_ref[...]
    q_ref[...] = (_rope(qn, cos, sin) * (A_HEAD_DIM ** -0.5)).astype(BF16)
    kn = k * lax.rsqrt(_group_mean_sq(k, gmat64[:LANES, :LANES]) + EPS) * gk_ref[...]
    k_ref[...] = _rope(kn, cos, sin).astype(BF16)
    v_ref[...] = v.astype(BF16)

    ge = 0.5 * z * (1.0 + jnp.tanh(np.sqrt(2.0 / np.pi).astype(np.float32)
                                   * (z + 0.044715 * (z * z * z))))
    u = ge[:, :MIX_HALF]
    vv = ge[:, MIX_HALF:]
    vn = vv * lax.rsqrt(_group_mean_sq(vv, gmat64) + EPS) * sgug_ref[...]
    vnb = vn.astype(BF16)
    lane = lax.broadcasted_iota(jnp.int32, (B_CHUNK, LANES), 1)
    low = lane < B_GROUP_DIM
    rows = []
    for c in range(x.shape[0] // B_CHUNK):
        slabs = []
        for s in range(MIX_HALF // LANES):
            vs = vnb[c * B_CHUNK:(c + 1) * B_CHUNK, s * LANES:(s + 1) * LANES]
            r0 = _dot(sguw_ref[2 * s], vs)
            r1 = _dot(sguw_ref[2 * s + 1], vs)
            slabs.append(jnp.where(low, r0, r1))
        rows.append(jnp.concatenate(slabs, axis=-1) + sgub_ref[...])
    sv = jnp.concatenate(rows, axis=0)
    mb_ref[...] = (u * sv).astype(BF16)


def _even_in_call(xs, mod, layer, g1, w_in, gq, gk, cos, sin, gmat, gmat64, sgug, sguw, sgub,
                  n_ctx_rows):
    b, t, d = xs.shape
    nt = t // TM
    nq = A_Q_HEADS * LANES
    full = lambda a: pl.BlockSpec(a.shape, lambda i, j: (0,) * a.ndim)
    return pl.pallas_call(
        _even_in_kernel,
        out_shape=(
            jax.ShapeDtypeStruct((b, t, nq), BF16),
            jax.ShapeDtypeStruct((b, t, LANES), BF16),
            jax.ShapeDtypeStruct((b, t, LANES), BF16),
            jax.ShapeDtypeStruct((b, t, MIX_HALF), BF16),
        ),
        grid=(b, nt),
        in_specs=[
            pl.BlockSpec((None, TM, d), lambda i, j: (i, j, 0)),
            pl.BlockSpec((None, None, N_MOD, d),
                         lambda i, j: (layer, jnp.where(j < n_ctx_rows // TM, b, i), 0, 0)),
            full(g1),
            pl.BlockSpec(memory_space=pltpu.VMEM),
            full(gq), full(gk),
            pl.BlockSpec((TM, LANES), lambda i, j: (j, 0)),
            pl.BlockSpec((TM, LANES), lambda i, j: (j, 0)),
            full(gmat), full(gmat64), full(sgug), full(sguw), full(sgub),
        ],
        out_specs=(
            pl.BlockSpec((None, TM, nq), lambda i, j: (i, j, 0)),
            pl.BlockSpec((None, TM, LANES), lambda i, j: (i, j, 0)),
            pl.BlockSpec((None, TM, LANES), lambda i, j: (i, j, 0)),
            pl.BlockSpec((None, TM, MIX_HALF), lambda i, j: (i, j, 0)),
        ),
        compiler_params=pltpu.CompilerParams(
            dimension_semantics=("arbitrary", "arbitrary"), vmem_limit_bytes=VMEM_LIMIT),
        name="even_in_proj",
    )(xs, mod, g1, w_in, gq, gk, cos, sin, gmat, gmat64, sgug, sguw, sgub)


def _softmax_pv(s, v):
    m = jnp.max(s, axis=-1, keepdims=True)
    p = jnp.exp(s - m)
    l = jnp.sum(p, axis=-1, keepdims=True)
    return _dot(p.astype(BF16), v) / l


def _even_attn_body(q_ref, k_ref, v_ref, o_ref, n_keys):
    k = k_ref[:n_keys, :]
    v = v_ref[:n_keys, :]
    lane = lax.broadcasted_iota(jnp.int32, (q_ref.shape[0], LANES), 1)
    low = lane < A_HEAD_DIM
    for pair in range(A_Q_HEADS // 2):
        outs = []
        for h in (2 * pair, 2 * pair + 1):
            s = _dot_nt(q_ref[:, h * LANES:(h + 1) * LANES], k)
            outs.append(_softmax_pv(s, v))
        kv_half = (2 * pair) // A_GROUP
        o0, o1 = outs
        if kv_half == 0:
            o1 = pltpu.roll(o1, A_HEAD_DIM, 1)
        else:
            o0 = pltpu.roll(o0, A_HEAD_DIM, 1)
        o_ref[:, pair * LANES:(pair + 1) * LANES] = jnp.where(low, o0, o1).astype(BF16)


def _even_attn_kernel(q_ref, k_ref, v_ref, o_ref, *, n_ctx, ctx_tiles):
    j = pl.program_id(1)

    @pl.when(j < ctx_tiles)
    def _():
        _even_attn_body(q_ref, k_ref, v_ref, o_ref, n_ctx)

    @pl.when(j >= ctx_tiles)
    def _():
        _even_attn_body(q_ref, k_ref, v_ref, o_ref, k_ref.shape[0])


def _odd_attn_body(q_ref, k_ref, v_ref, o_ref, n_keys):
    lane = lax.broadcasted_iota(jnp.int32, (q_ref.shape[0], LANES), 1)
    low = lane < C_V
    for pair in range(C_HEADS // 2):
        v = v_ref[:n_keys, pair * LANES:(pair + 1) * LANES]
        outs = []
        for h in (2 * pair, 2 * pair + 1):
            s = _dot_nt(q_ref[:, h * LANES:(h + 1) * LANES], k_ref[:n_keys, h * LANES:(h + 1) * LANES])
            outs.append(_softmax_pv(s, v))
        o_ref[:, pair * LANES:(pair + 1) * LANES] = jnp.where(low, outs[0], outs[1]).astype(BF16)


def _odd_attn_kernel(q_ref, k_ref, v_ref, o_ref, *, n_ctx, ctx_tiles, tile0):
    j = pl.program_id(1) + tile0

    @pl.when(j < ctx_tiles)
    def _():
        _odd_attn_body(q_ref, k_ref, v_ref, o_ref, n_ctx)

    @pl.when(j >= ctx_tiles)
    def _():
        _odd_attn_body(q_ref, k_ref, v_ref, o_ref, k_ref.shape[0])


def _attn_call(kernel_fn, q, k, v, n_ctx, tile0, name):
    b, t, nq = q.shape
    nt = t // TM - tile0
    return pl.pallas_call(
        kernel_fn,
        out_shape=jax.ShapeDtypeStruct((b, nt * TM, MIX_HALF), BF16),
        grid=(b, nt),
        in_specs=[
            pl.BlockSpec((None, TM, nq), lambda i, j: (i, j + tile0, 0)),
            pl.BlockSpec((None, t, k.shape[-1]), lambda i, j: (i, 0, 0)),
            pl.BlockSpec((None, t, v.shape[-1]), lambda i, j: (i, 0, 0)),
        ],
        out_specs=pl.BlockSpec((None, TM, MIX_HALF), lambda i, j: (i, j, 0)),
        compiler_params=pltpu.CompilerParams(
            dimension_semantics=("arbitrary", "arbitrary"), vmem_limit_bytes=VMEM_LIMIT),
        name=name,
    )(q, k, v)


def _odd_in_kernel(x_ref, mod_ref, g1_ref, w_ref, gcq_ref, gckv_ref, wuq_ref, wukv_ref,
                   cos_ref, sin_ref, q_ref, k_ref, v_ref, y_ref):
    x = x_ref[...]
    h = _modulate(x, g1_ref[...], mod_ref[0:1, :], mod_ref[1:2, :])
    p = _dot(h.astype(BF16), w_ref[...])
    cq = p[:, :C_Q_RANK]
    ckv = p[:, C_Q_RANK:C_Q_RANK + C_KV_RANK]
    o = C_Q_RANK + C_KV_RANK
    kr = p[:, o:o + LANES]
    z = p[:, o + LANES:]
    cos = cos_ref[...]
    sin = sin_ref[...]

    q = _dot((_rms(cq) * gcq_ref[...]).astype(BF16), wuq_ref[...])
    q_ref[...] = (_rope(q, cos, sin) * ((C_NOPE + C_ROPE) ** -0.5)).astype(BF16)
    kv = _dot((_rms(ckv) * gckv_ref[...]).astype(BF16), wukv_ref[...])
    nk = C_HEADS * LANES
    krr = _rope(kr, cos, sin)
    k_ref[...] = (kv[:, :nk] + jnp.concatenate([krr] * C_HEADS, axis=-1)).astype(BF16)
    v_ref[...] = kv[:, nk:].astype(BF16)
    a = z[:, :MIX_HALF]
    gt = z[:, MIX_HALF:]
    y_ref[...] = a * (1.0 / (1.0 + jnp.exp(-gt)))


def _odd_in_call(xs, mod, layer, g1, w_in, gcq, gckv, wuq, wukv, cos, sin, n_ctx_rows):
    b, t, d = xs.shape
    nt = t // TM
    nk = C_HEADS * LANES
    full = lambda a: pl.BlockSpec(a.shape, lambda i, j: (0,) * a.ndim)
    return pl.pallas_call(
        _odd_in_kernel,
        out_shape=(
            jax.ShapeDtypeStruct((b, t, nk), BF16),
            jax.ShapeDtypeStruct((b, t, nk), BF16),
            jax.ShapeDtypeStruct((b, t, MIX_HALF), BF16),
            jax.ShapeDtypeStruct((b, t, MIX_HALF), F32),
        ),
        grid=(b, nt),
        in_specs=[
            pl.BlockSpec((None, TM, d), lambda i, j: (i, j, 0)),
            pl.BlockSpec((None, None, N_MOD, d),
                         lambda i, j: (layer, jnp.where(j < n_ctx_rows // TM, b, i), 0, 0)),
            full(g1),
            pl.BlockSpec(memory_space=pltpu.VMEM),
            full(gcq), full(gckv),
            pl.BlockSpec(memory_space=pltpu.VMEM),
            pl.BlockSpec(memory_space=pltpu.VMEM),
            pl.BlockSpec((TM, LANES), lambda i, j: (j, 0)),
            pl.BlockSpec((TM, LANES), lambda i, j: (j, 0)),
        ],
        out_specs=(
            pl.BlockSpec((None, TM, nk), lambda i, j: (i, j, 0)),
            pl.BlockSpec((None, TM, nk), lambda i, j: (i, j, 0)),
            pl.BlockSpec((None, TM, MIX_HALF), lambda i, j: (i, j, 0)),
            pl.BlockSpec((None, TM, MIX_HALF), lambda i, j: (i, j, 0)),
        ),
        compiler_params=pltpu.CompilerParams(
            dimension_semantics=("arbitrary", "arbitrary"), vmem_limit_bytes=VMEM_LIMIT),
        name="odd_in_proj",
    )(xs, mod, g1, w_in, gcq, gckv, wuq, wukv, cos, sin)


def _out_mlp(x, attn, mixb, mod_ref, g2_ref, wo_ref, w1_ref, w2_ref):
    a = _dot(attn, wo_ref[:MIX_HALF, :]) + _dot(mixb, wo_ref[MIX_HALF:, :])
    x1 = x + mod_ref[2:3, :] * a
    h2 = _modulate(x1, g2_ref[...], mod_ref[3:4, :], mod_ref[4:5, :]).astype(BF16)
    acc = None
    for c in range(0, FF_DIM, FF_CHUNK):
        hc = jnp.maximum(_dot(h2, w1_ref[:, c:c + FF_CHUNK]), 0.0)
        part = _dot((hc * hc).astype(BF16), w2_ref[c:c + FF_CHUNK, :])
        acc = part if acc is None else acc + part
    return x1 + mod_ref[5:6, :] * acc


def _even_out_kernel(x_ref, attn_ref, mb_ref, mod_ref, g2_ref, wo_ref, w1_ref, w2_ref, o_ref):
    o_ref[...] = _out_mlp(x_ref[...], attn_ref[...], mb_ref[...], mod_ref, g2_ref, wo_ref, w1_ref,
                          w2_ref)


def _odd_out_kernel(x_ref, attn_ref, yp_ref, yc_ref, yn_ref, mod_ref, g2_ref, wo_ref, w1_ref, w2_ref,
                    cw_ref, cb_ref, lng_ref, lnb_ref, fg_ref, o_ref, ybuf, *,
                    tile0, first_latent, last_tile, final_norm):
    j = pl.program_id(1) + tile0
    has_prev = jnp.logical_and(j > 0, j != first_latent)
    has_next = jnp.logical_and(j < last_tile, j != first_latent - 1)
    ybuf[0:HALO, :] = jnp.where(has_prev, yp_ref[...], 0.0)
    ybuf[HALO:HALO + TM, :] = yc_ref[...]
    ybuf[HALO + TM:, :] = jnp.where(has_next, yn_ref[...], 0.0)
    base = HALO - D_CONV // 2
    acc = None
    for tap in range(D_CONV):
        term = ybuf[base + tap:base + tap + TM, :] * cw_ref[tap:tap + 1, :]
        acc = term if acc is None else acc + term
    y = acc + cb_ref[...]
    mu = jnp.mean(y, axis=-1, keepdims=True)
    yc = y - mu
    var = jnp.mean(yc * yc, axis=-1, keepdims=True)
    ln = yc * lax.rsqrt(var + EPS) * lng_ref[...] + lnb_ref[...]
    mixb = (ln * (1.0 / (1.0 + jnp.exp(-ln)))).astype(BF16)
    out = _out_mlp(x_ref[...], attn_ref[...], mixb, mod_ref, g2_ref, wo_ref, w1_ref, w2_ref)
    if final_norm:
        out = _rms(out) * fg_ref[...]
    o_ref[...] = out


def _out_call(xs, attn, mixb_or_y, mod, layer, g2, wo, w1, w2, n_ctx_rows, tile0, conv=None):
    b, t, d = xs.shape
    nt = t // TM - tile0
    ctx_tiles = n_ctx_rows // TM
    full = lambda a: pl.BlockSpec(a.shape, lambda i, j: (0,) * a.ndim)
    whole = pl.BlockSpec(memory_space=pltpu.VMEM)
    x_spec = pl.BlockSpec((None, TM, d), lambda i, j: (i, j + tile0, 0))
    attn_spec = pl.BlockSpec((None, TM, MIX_HALF), lambda i, j: (i, j, 0))
    mod_spec = pl.BlockSpec((None, None, N_MOD, d),
                            lambda i, j: (layer, jnp.where(j + tile0 < ctx_tiles, b, i), 0, 0))
    out_spec = pl.BlockSpec((None, TM, d), lambda i, j: (i, j, 0))
    params = pltpu.CompilerParams(
        dimension_semantics=("arbitrary", "arbitrary"), vmem_limit_bytes=VMEM_LIMIT)
    out_shape = jax.ShapeDtypeStruct((b, nt * TM, d), F32)
    if conv is None:
        return pl.pallas_call(
            _even_out_kernel,
            out_shape=out_shape,
            grid=(b, nt),
            in_specs=[x_spec, attn_spec,
                      pl.BlockSpec((None, TM, MIX_HALF), lambda i, j: (i, j + tile0, 0)),
                      mod_spec, full(g2), whole, whole, whole],
            out_specs=out_spec,
            compiler_params=params,
            name="even_out_mlp",
        )(xs, attn, mixb_or_y, mod, g2, wo, w1, w2)
    cw, cb, lng, lnb, fg, final_norm = conv
    hb = TM // HALO
    last_halo = t // HALO - 1
    kern = functools.partial(_odd_out_kernel, tile0=tile0, first_latent=ctx_tiles,
                             last_tile=t // TM - 1, final_norm=final_norm)
    return pl.pallas_call(
        kern,
        out_shape=out_shape,
        grid=(b, nt),
        in_specs=[x_spec, attn_spec,
                  pl.BlockSpec((None, HALO, MIX_HALF),
                               lambda i, j: (i, jnp.maximum((j + tile0) * hb - 1, 0), 0)),
                  pl.BlockSpec((None, TM, MIX_HALF), lambda i, j: (i, j + tile0, 0)),
                  pl.BlockSpec((None, HALO, MIX_HALF),
                               lambda i, j: (i, jnp.minimum((j + tile0 + 1) * hb, last_halo), 0)),
                  mod_spec, full(g2), whole, whole, whole,
                  full(cw), full(cb), full(lng), full(lnb), full(fg)],
        out_specs=out_spec,
        scratch_shapes=[pltpu.VMEM((TM + 2 * HALO, MIX_HALF), F32)],
        compiler_params=params,
        name="odd_out_mlp",
    )(xs, attn, mixb_or_y, mixb_or_y, mixb_or_y, mod, g2, wo, w1, w2, cw, cb, lng, lnb, fg)


def _axial_angles(length, d_rot):
    rows = length // GRID_W
    row = jnp.broadcast_to(jnp.arange(rows)[:, None], (rows, GRID_W)).reshape(-1).astype(F32)
    col = jnp.broadcast_to(jnp.arange(GRID_W)[None, :], (rows, GRID_W)).reshape(-1).astype(F32)
    d_axis = d_rot // 2
    inv = ROPE_THETA ** (-jnp.arange(0, d_axis, 2, dtype=F32) / d_axis)
    return jnp.concatenate([row[:, None] * inv, col[:, None] * inv], axis=-1)


def _rope_tables(seq, n_ctx, d_rot, lane0, period):
    ang = _axial_angles(seq, d_rot)
    cos = jnp.repeat(jnp.cos(ang), 2, axis=-1)
    sin = jnp.repeat(jnp.sin(ang), 2, axis=-1) * jnp.tile(jnp.array([-1.0, 1.0], F32), d_rot // 2)
    cos_p = jnp.ones((seq, period), F32).at[:, lane0:lane0 + d_rot].set(cos)
    sin_p = jnp.zeros((seq, period), F32).at[:, lane0:lane0 + d_rot].set(sin)
    cos_p = jnp.tile(cos_p, (1, LANES // period))
    sin_p = jnp.tile(sin_p, (1, LANES // period))
    cos_t = jnp.concatenate([jnp.ones((n_ctx, LANES), F32), cos_p], axis=0)
    sin_t = jnp.concatenate([jnp.zeros((n_ctx, LANES), F32), sin_p], axis=0)
    return cos_t, sin_t


def _even_weights(w_in, q_g, k_g, sgu_g, sgu_w, sgu_b):
    d = w_in.shape[0]
    ev_q = A_Q_HEADS * A_HEAD_DIM
    ev_kv = A_KV_HEADS * A_HEAD_DIM
    wq = w_in[:, :ev_q].reshape(d, A_Q_HEADS, A_HEAD_DIM)
    slabs = []
    gq = []
    zero = jnp.zeros((d, A_HEAD_DIM), w_in.dtype)
    zg = jnp.zeros((A_HEAD_DIM,), q_g.dtype)
    for h in range(A_Q_HEADS):
        if h // A_GROUP == 0:
            slabs += [wq[:, h], zero]
            gq += [q_g, zg]
        else:
            slabs += [zero, wq[:, h]]
            gq += [zg, q_g]
    w = jnp.concatenate(slabs + [w_in[:, ev_q:]], axis=1).astype(BF16)
    gq = jnp.concatenate(gq)[None, :]
    gk = jnp.tile(k_g, A_KV_HEADS)[None, :]
    sgug = sgu_g.reshape(1, -1)
    sgub = jnp.repeat(sgu_b.T, B_GROUP_DIM, axis=1)
    return w, gq, gk, sgug, sgu_w.astype(BF16), sgub


def _odd_weights(w_in, w_uq, w_ukv):
    d = w_in.shape[0]
    c0 = C_Q_RANK + C_KV_RANK
    w_kr = w_in[:, c0:c0 + C_ROPE]
    kr_slab = jnp.zeros((d, LANES), w_in.dtype).at[:, C_NOPE:C_NOPE + C_ROPE].set(w_kr)
    w = jnp.concatenate([w_in[:, :c0], kr_slab, w_in[:, c0 + C_ROPE:]], axis=1).astype(BF16)
    uq = w_uq.reshape(C_Q_RANK, C_HEADS, C_NOPE + C_ROPE)
    uq = jnp.pad(uq, ((0, 0), (0, 0), (0, LANES - C_NOPE - C_ROPE))).reshape(C_Q_RANK, C_HEADS * LANES)
    ukv = w_ukv.reshape(C_KV_RANK, C_HEADS, C_NOPE + C_V)
    uk = jnp.pad(ukv[:, :, :C_NOPE], ((0, 0), (0, 0), (0, LANES - C_NOPE))).reshape(C_KV_RANK, -1)
    uv = ukv[:, :, C_NOPE:].reshape(C_KV_RANK, C_HEADS * C_V)
    return w, uq.astype(BF16), jnp.concatenate([uk, uv], axis=1).astype(BF16)


def _group_matrix(group_lanes):
    idx = np.arange(2 * LANES) // group_lanes
    return jnp.asarray((idx[:, None] == idx[None, :]).astype(np.float32) / A_HEAD_DIM, BF16)


def kernel(x, c, ctx, c_ctx, ada_w, ada_b, norm1_g, norm2_g, w_out, mlp_w1, mlp_w2, ev_w_in, ev_q_norm_g, ev_k_norm_g, ev_sgu_norm_g, ev_sgu_w, ev_sgu_b, od_w_in, od_q_norm_g, od_kv_norm_g, od_w_uq, od_w_ukv, od_conv_w, od_conv_b, od_ln_g, od_ln_b, final_g):
    b, seq, d = x.shape
    n_ctx = ctx.shape[1]
    depth = ada_w.shape[0]
    assert d == D_MODEL and seq % TM == 0 and n_ctx % TM == 0 and depth == 2
    ctx_tiles = n_ctx // TM

    xs = jnp.concatenate([ctx, x], axis=1)
    r = -(-(b + 1) // 8) * 8
    cvec = jnp.concatenate([c, c_ctx[None, :], jnp.zeros((r - b - 1, d), F32)], axis=0)
    mod = _ada_call(cvec, ada_w, ada_b).reshape(depth, r, N_MOD, d)

    wo = w_out.astype(BF16)
    w1 = mlp_w1.astype(BF16)
    w2 = mlp_w2.astype(BF16)

    cos_e, sin_e = _rope_tables(seq, n_ctx, A_HEAD_DIM, 0, A_HEAD_DIM)
    w_e, gq, gk, sgug, sguw, sgub = _even_weights(ev_w_in[0], ev_q_norm_g[0], ev_k_norm_g[0],
                                                  ev_sgu_norm_g[0], ev_sgu_w[0], ev_sgu_b[0])
    q, k, v, mixb = _even_in_call(xs, mod, 0, norm1_g[0][None, :], w_e, gq, gk, cos_e, sin_e,
                                  _group_matrix(LANES), _group_matrix(A_HEAD_DIM), sgug, sguw, sgub,
                                  n_ctx)
    attn = _attn_call(functools.partial(_even_attn_kernel, n_ctx=n_ctx, ctx_tiles=ctx_tiles),
                      q, k, v, n_ctx, 0, "even_attention")
    xs = _out_call(xs, attn, mixb, mod, 0, norm2_g[0][None, :], wo[0], w1[0], w2[0], n_ctx, 0)

    cos_o, sin_o = _rope_tables(seq, n_ctx, C_ROPE, C_NOPE, LANES)
    w_o, uq, ukv = _odd_weights(od_w_in[0], od_w_uq[0], od_w_ukv[0])
    q, k, v, y = _odd_in_call(xs, mod, 1, norm1_g[1][None, :], w_o, od_q_norm_g[0][None, :],
                              od_kv_norm_g[0][None, :], uq, ukv, cos_o, sin_o, n_ctx)
    attn = _attn_call(functools.partial(_odd_attn_kernel, n_ctx=n_ctx, ctx_tiles=ctx_tiles,
                                        tile0=ctx_tiles),
                      q, k, v, n_ctx, ctx_tiles, "odd_attention")
    conv = (od_conv_w[0], od_conv_b[0][None, :], od_ln_g[0][None, :], od_ln_b[0][None, :],
            final_g[None, :], True)
    return _out_call(xs, attn, y, mod, 1, norm2_g[1][None, :], wo[1], w1[1], w2[1], n_ctx,
                     ctx_tiles, conv=conv)
```

```python
import functools

import numpy as np
import jax
import jax.numpy as jnp
from jax import lax
from jax.experimental import pallas as pl
from jax.experimental.pallas import tpu as pltpu

F32 = jnp.float32
BF16 = jnp.bfloat16

D_MODEL = 1024
GRID_W = 64
ROPE_THETA = 10000.0
EPS = 1e-6
MIX_HALF = D_MODEL // 2
N_MOD = 6

A_HEAD_DIM = 64
A_Q_HEADS = 8
A_KV_HEADS = 2
A_GROUP = A_Q_HEADS // A_KV_HEADS
B_GROUPS = 8
B_GROUP_DIM = 64
B_CHUNK = 128
C_HEADS = 8
C_NOPE = 64
C_ROPE = 32
C_V = 64
C_Q_RANK = 256
C_KV_RANK = 128
D_CONV = 31
FF_DIM = 4 * D_MODEL

LANES = 128
TM = 256
FF_CHUNK = 1024
SUBLANES = 8
OUT_SAMPLES = 2
CONV_ROWS = 32
HALO = 16
VMEM_LIMIT = 56 * 1024 * 1024
LOG2E = 1.4426950408889634


def _dot(a, b):
    return jnp.dot(a, b, preferred_element_type=F32)


def _dot_nt(a, b):
    return lax.dot_general(a, b, (((1,), (1,)), ((), ())), preferred_element_type=F32)


def _rms(x):
    return x * lax.rsqrt(jnp.mean(x * x, axis=-1, keepdims=True) + EPS)


def _modulate(x, g, shift, scale):
    return (_rms(x) * g) * (1.0 + scale) + shift


def _pair_swap(x):
    even = lax.broadcasted_iota(jnp.int32, (x.shape[0], LANES), 1) % 2 == 0
    slabs = []
    for s in range(0, x.shape[-1], LANES):
        xs = x[:, s:s + LANES]
        slabs.append(jnp.where(even, pltpu.roll(xs, LANES - 1, 1), pltpu.roll(xs, 1, 1)))
    return slabs[0] if len(slabs) == 1 else jnp.concatenate(slabs, axis=-1)


def _rope(x, cos, sin_signed):
    reps = x.shape[-1] // LANES
    if reps > 1:
        cos = jnp.concatenate([cos] * reps, axis=-1)
        sin_signed = jnp.concatenate([sin_signed] * reps, axis=-1)
    return x * cos + _pair_swap(x) * sin_signed


def _group_mean_sq(x, gmat):
    w = gmat.shape[0]
    x2 = (x * x).astype(BF16)
    parts = [_dot(x2[:, i:i + w], gmat) for i in range(0, x.shape[-1], w)]
    return parts[0] if len(parts) == 1 else jnp.concatenate(parts, axis=-1)


def _ada_kernel(c_ref, w_ref, b_ref, o_ref):
    c = c_ref[...]
    s = c * (1.0 / (1.0 + jnp.exp(-c)))
    o_ref[...] = _dot(s.astype(BF16), w_ref[...].astype(BF16)) + b_ref[...]


def _ada_call(cvec, ada_w, ada_b):
    depth, d, n = ada_w.shape
    r = cvec.shape[0]
    bn = 1536
    return pl.pallas_call(
        _ada_kernel,
        out_shape=jax.ShapeDtypeStruct((depth, r, n), F32),
        grid=(depth, n // bn),
        in_specs=[
            pl.BlockSpec((r, d), lambda i, j: (0, 0)),
            pl.BlockSpec((None, d, bn), lambda i, j: (i, 0, j)),
            pl.BlockSpec((None, 1, bn), lambda i, j: (i, 0, j)),
        ],
        out_specs=pl.BlockSpec((None, r, bn), lambda i, j: (i, 0, j)),
        compiler_params=pltpu.CompilerParams(
            dimension_semantics=("arbitrary", "arbitrary"), vmem_limit_bytes=VMEM_LIMIT),
        name="ada_mod",
    )(cvec, ada_w, ada_b.reshape(depth, 1, n))


def _even_in_kernel(x_ref, mod_ref, g1_ref, w_ref, gq_ref, gk_ref, cos_ref, sin_ref, gmat_ref,
                    gmat64_ref, sgug_ref, sguw_ref, sgub_ref, q_ref, k_ref, v_ref, mb_ref):
    x = x_ref[...]
    h = _modulate(x, g1_ref[...], mod_ref[0:1, :], mod_ref[1:2, :])
    p = _dot(h.astype(BF16), w_ref[...])
    nq = A_Q_HEADS * LANES
    q = p[:, :nq]
    k = p[:, nq:nq + LANES]
    v = p[:, nq + LANES:nq + 2 * LANES]
    z = p[:, nq + 2 * LANES:]
    gmat = gmat_ref[...]
    gmat64 = gmat64_ref[...]
    cos = cos_ref[...]
    sin = sin_ref[...]

    qn = q * lax.rsqrt(_group_mean_sq(q, gmat) + EPS) * gq_ref[...]
    q_ref[...] = (_rope(qn, cos, sin) * (A_HEAD_DIM ** -0.5 * LOG2E)).astype(BF16)
    kn = k * lax.rsqrt(_group_mean_sq(k, gmat64[:LANES, :LANES]) + EPS) * gk_ref[...]
    k_ref[...] = _rope(kn, cos, sin).astype(BF16)
    low = lax.broadcasted_iota(jnp.int32, v.shape, 1) < A_HEAD_DIM
    v_ref[:, :LANES] = jnp.where(low, v, 1.0).astype(BF16)
    v_ref[:, LANES:] = jnp.where(low, pltpu.roll(v, A_HEAD_DIM, 1), 1.0).astype(BF16)

    ge = 0.5 * z * (1.0 + jnp.tanh(np.sqrt(2.0 / np.pi).astype(np.float32)
                                   * (z + 0.044715 * (z * z * z))))
    u = ge[:, :MIX_HALF]
    vv = ge[:, MIX_HALF:]
    vn = vv * lax.rsqrt(_group_mean_sq(vv, gmat64) + EPS) * sgug_ref[...]
    vnb = vn.astype(BF16)
    lane = lax.broadcasted_iota(jnp.int32, (B_CHUNK, LANES), 1)
    low = lane < B_GROUP_DIM
    rows = []
    for c in range(x.shape[0] // B_CHUNK):
        slabs = []
        for s in range(MIX_HALF // LANES):
            vs = vnb[c * B_CHUNK:(c + 1) * B_CHUNK, s * LANES:(s + 1) * LANES]
            r0 = _dot(sguw_ref[2 * s], vs)
            r1 = _dot(sguw_ref[2 * s + 1], vs)
            slabs.append(jnp.where(low, r0, r1))
        rows.append(jnp.concatenate(slabs, axis=-1) + sgub_ref[...])
    sv = jnp.concatenate(rows, axis=0)
    mb_ref[...] = (u * sv).astype(BF16)


def _even_in_call(xs, mod, layer, g1, w_in, gq, gk, cos, sin, gmat, gmat64, sgug, sguw, sgub,
                  n_ctx_rows):
    b, t, d = xs.shape
    nt = t // TM
    nq = A_Q_HEADS * LANES
    full = lambda a: pl.BlockSpec(a.shape, lambda i, j: (0,) * a.ndim)
    return pl.pallas_call(
        _even_in_kernel,
        out_shape=(
            jax.ShapeDtypeStruct((b, t, nq), BF16),
            jax.ShapeDtypeStruct((b, t, LANES), BF16),
            jax.ShapeDtypeStruct((b, t, A_KV_HEADS * LANES), BF16),
            jax.ShapeDtypeStruct((b, t, MIX_HALF), BF16),
        ),
        grid=(b, nt),
        in_specs=[
            pl.BlockSpec((None, TM, d), lambda i, j: (i, j, 0)),
            pl.BlockSpec((None, None, N_MOD, d),
                         lambda i, j: (layer, jnp.where(j < n_ctx_rows // TM, b, i), 0, 0)),
            full(g1),
            pl.BlockSpec(memory_space=pltpu.VMEM),
            full(gq), full(gk),
            pl.BlockSpec((TM, LANES), lambda i, j: (j, 0)),
            pl.BlockSpec((TM, LANES), lambda i, j: (j, 0)),
            full(gmat), full(gmat64), full(sgug), full(sguw), full(sgub),
        ],
        out_specs=(
            pl.BlockSpec((None, TM, nq), lambda i, j: (i, j, 0)),
            pl.BlockSpec((None, TM, LANES), lambda i, j: (i, j, 0)),
            pl.BlockSpec((None, TM, A_KV_HEADS * LANES), lambda i, j: (i, j, 0)),
            pl.BlockSpec((None, TM, MIX_HALF), lambda i, j: (i, j, 0)),
        ),
        compiler_params=pltpu.CompilerParams(
            dimension_semantics=("arbitrary", "arbitrary"), vmem_limit_bytes=VMEM_LIMIT),
        name="even_in_proj",
    )(xs, mod, g1, w_in, gq, gk, cos, sin, gmat, gmat64, sgug, sguw, sgub)


def _attn_body(q_ref, k_ref, v_ref, o_ref, n_keys, groups):
    tm = q_ref.shape[0]
    half = LANES // 2
    low = lax.broadcasted_iota(jnp.int32, (tm, LANES), 1) < half
    res = {}
    for heads, ks, vs in groups:
        q = [q_ref[:, h * LANES:(h + 1) * LANES] for h in heads]
        q = q[0] if len(q) == 1 else jnp.concatenate(q, axis=0)
        s = _dot_nt(q, k_ref[:n_keys, ks * LANES:(ks + 1) * LANES])
        p = jnp.exp2(s - jnp.max(s, axis=-1, keepdims=True))
        pv = _dot(p.astype(BF16), v_ref[:n_keys, vs * LANES:(vs + 1) * LANES])
        sw = pltpu.roll(pv, half, 1)
        for i, h in enumerate(heads):
            a, b = pv[i * tm:(i + 1) * tm], sw[i * tm:(i + 1) * tm]
            res[h] = a / b if h % 2 == 0 else b / a
    for pair in range(len(res) // 2):
        o_ref[:, pair * LANES:(pair + 1) * LANES] = jnp.where(
            low, res[2 * pair], res[2 * pair + 1]).astype(BF16)


def _conv_ln_silu(yp_ref, yc_ref, yn_ref, has_prev, has_next, ybuf, ysh, cw_ref, cb_ref, lng_ref,
                  lnb_ref, o_ref):
    tm = yc_ref.shape[0]
    ybuf[0:HALO, :] = jnp.where(has_prev, yp_ref[...], 0.0)
    ybuf[HALO:HALO + tm, :] = yc_ref[...]
    ybuf[HALO + tm:, :] = jnp.where(has_next, yn_ref[...], 0.0)
    n_sh = ysh.shape[1]
    for s in range(1, SUBLANES):
        ysh[s - 1] = ybuf[s:s + n_sh, :]
    base = HALO - D_CONV // 2
    nch = yc_ref.shape[1]
    for r0 in range(0, tm, CONV_ROWS):
        acc = None
        for tap in range(D_CONV):
            a, s = divmod(base + tap, SUBLANES)
            lo = SUBLANES * a + r0
            win = ybuf[lo:lo + CONV_ROWS, :] if s == 0 else ysh[s - 1, lo:lo + CONV_ROWS, :]
            term = win.reshape(CONV_ROWS // SUBLANES, SUBLANES, nch) * cw_ref[tap]
            acc = term if acc is None else acc + term
        y = acc.reshape(CONV_ROWS, nch) + cb_ref[...]
        mu = jnp.mean(y, axis=-1, keepdims=True)
        yc = y - mu
        var = jnp.mean(yc * yc, axis=-1, keepdims=True)
        ln = yc * lax.rsqrt(var + EPS) * lng_ref[...] + lnb_ref[...]
        o_ref[r0:r0 + CONV_ROWS, :] = (ln * (1.0 / (1.0 + jnp.exp(-ln)))).astype(BF16)


def _attn_kernel(q_ref, k_ref, v_ref, *rest, n_ctx, ctx_tiles, tile0, last_tile, groups, conv):
    j = pl.program_id(1) + tile0
    o_ref = rest[7] if conv else rest[0]

    def body(n_keys):
        if conv:
            yp_ref, yc_ref, yn_ref, cw_ref, cb_ref, lng_ref, lnb_ref, _, mb_ref, ybuf, ysh = rest
            has_prev = jnp.logical_and(j > 0, j != ctx_tiles)
            has_next = jnp.logical_and(j < last_tile, j != ctx_tiles - 1)
            _conv_ln_silu(yp_ref, yc_ref, yn_ref, has_prev, has_next, ybuf, ysh, cw_ref, cb_ref,
                          lng_ref, lnb_ref, mb_ref)
        _attn_body(q_ref, k_ref, v_ref, o_ref, n_keys, groups)

    @pl.when(j < ctx_tiles)
    def _():
        body(n_ctx)

    @pl.when(j >= ctx_tiles)
    def _():
        body(k_ref.shape[0])


def _attn_call(q, k, v, n_ctx, tile0, groups, name, conv=None):
    b, t, nq = q.shape
    nt = t // TM - tile0
    kernel_fn = functools.partial(_attn_kernel, n_ctx=n_ctx, ctx_tiles=n_ctx // TM, tile0=tile0,
                                  last_tile=t // TM - 1, groups=groups, conv=conv is not None)
    in_specs = [
        pl.BlockSpec((None, TM, nq), lambda i, j: (i, j + tile0, 0)),
        pl.BlockSpec((None, t, k.shape[-1]), lambda i, j: (i, 0, 0)),
        pl.BlockSpec((None, t, v.shape[-1]), lambda i, j: (i, 0, 0)),
    ]
    o_shape = jax.ShapeDtypeStruct((b, nt * TM, MIX_HALF), BF16)
    o_spec = pl.BlockSpec((None, TM, MIX_HALF), lambda i, j: (i, j, 0))
    params = pltpu.CompilerParams(
        dimension_semantics=("arbitrary", "arbitrary"), vmem_limit_bytes=VMEM_LIMIT)
    if conv is None:
        return pl.pallas_call(kernel_fn, out_shape=o_shape, grid=(b, nt), in_specs=in_specs,
                              out_specs=o_spec, compiler_params=params, name=name)(q, k, v)
    y, cw, cb, lng, lnb = conv
    full = lambda a: pl.BlockSpec(a.shape, lambda i, j: (0,) * a.ndim)
    hb = TM // HALO
    last_halo = t // HALO - 1
    in_specs += [
        pl.BlockSpec((None, HALO, MIX_HALF),
                     lambda i, j: (i, jnp.maximum((j + tile0) * hb - 1, 0), 0)),
        pl.BlockSpec((None, TM, MIX_HALF), lambda i, j: (i, j + tile0, 0)),
        pl.BlockSpec((None, HALO, MIX_HALF),
                     lambda i, j: (i, jnp.minimum((j + tile0 + 1) * hb, last_halo), 0)),
        full(cw), full(cb), full(lng), full(lnb),
    ]
    return pl.pallas_call(
        kernel_fn, out_shape=(o_shape, o_shape), grid=(b, nt), in_specs=in_specs,
        out_specs=(o_spec, o_spec),
        scratch_shapes=[pltpu.VMEM((TM + 2 * HALO, MIX_HALF), F32),
                        pltpu.VMEM((SUBLANES - 1, TM + 2 * HALO - SUBLANES, MIX_HALF), F32)],
        compiler_params=params, name=name,
    )(q, k, v, y, y, y, cw, cb, lng, lnb)


def _odd_in_kernel(x_ref, mod_ref, g1_ref, w_ref, gcq_ref, gckv_ref, wuq_ref, wukv_ref,
                   cos_ref, sin_ref, q_ref, k_ref, v_ref, y_ref):
    x = x_ref[...]
    h = _modulate(x, g1_ref[...], mod_ref[0:1, :], mod_ref[1:2, :])
    p = _dot(h.astype(BF16), w_ref[...])
    cq = p[:, :C_Q_RANK]
    ckv = p[:, C_Q_RANK:C_Q_RANK + C_KV_RANK]
    o = C_Q_RANK + C_KV_RANK
    kr = p[:, o:o + LANES]
    z = p[:, o + LANES:]
    cos = cos_ref[...]
    sin = sin_ref[...]

    q = _dot((_rms(cq) * gcq_ref[...]).astype(BF16), wuq_ref[...])
    q_ref[...] = (_rope(q, cos, sin) * ((C_NOPE + C_ROPE) ** -0.5 * LOG2E)).astype(BF16)
    kv = _dot((_rms(ckv) * gckv_ref[...]).astype(BF16), wukv_ref[...])
    nk = C_HEADS * LANES
    krr = _rope(kr, cos, sin)
    k_ref[...] = (kv[:, :nk] + jnp.concatenate([krr] * C_HEADS, axis=-1)).astype(BF16)
    v = kv[:, nk:]
    high = lax.broadcasted_iota(jnp.int32, v.shape, 1) % LANES >= C_V
    v_ref[...] = jnp.where(high, 1.0, v).astype(BF16)
    a = z[:, :MIX_HALF]
    gt = z[:, MIX_HALF:]
    y_ref[...] = a * (1.0 / (1.0 + jnp.exp(-gt)))


def _odd_in_call(xs, mod, layer, g1, w_in, gcq, gckv, wuq, wukv, cos, sin, n_ctx_rows):
    b, t, d = xs.shape
    nt = t // TM
    nk = C_HEADS * LANES
    full = lambda a: pl.BlockSpec(a.shape, lambda i, j: (0,) * a.ndim)
    return pl.pallas_call(
        _odd_in_kernel,
        out_shape=(
            jax.ShapeDtypeStruct((b, t, nk), BF16),
            jax.ShapeDtypeStruct((b, t, nk), BF16),
            jax.ShapeDtypeStruct((b, t, nk), BF16),
            jax.ShapeDtypeStruct((b, t, MIX_HALF), F32),
        ),
        grid=(b, nt),
        in_specs=[
            pl.BlockSpec((None, TM, d), lambda i, j: (i, j, 0)),
            pl.BlockSpec((None, None, N_MOD, d),
                         lambda i, j: (layer, jnp.where(j < n_ctx_rows // TM, b, i), 0, 0)),
            full(g1),
            pl.BlockSpec(memory_space=pltpu.VMEM),
            full(gcq), full(gckv),
            pl.BlockSpec(memory_space=pltpu.VMEM),
            pl.BlockSpec(memory_space=pltpu.VMEM),
            pl.BlockSpec((TM, LANES), lambda i, j: (j, 0)),
            pl.BlockSpec((TM, LANES), lambda i, j: (j, 0)),
        ],
        out_specs=(
            pl.BlockSpec((None, TM, nk), lambda i, j: (i, j, 0)),
            pl.BlockSpec((None, TM, nk), lambda i, j: (i, j, 0)),
            pl.BlockSpec((None, TM, nk), lambda i, j: (i, j, 0)),
            pl.BlockSpec((None, TM, MIX_HALF), lambda i, j: (i, j, 0)),
        ),
        compiler_params=pltpu.CompilerParams(
            dimension_semantics=("arbitrary", "arbitrary"), vmem_limit_bytes=VMEM_LIMIT),
        name="odd_in_proj",
    )(xs, mod, g1, w_in, gcq, gckv, wuq, wukv, cos, sin)


def _out_kernel(x_ref, attn_ref, mb_ref, mod_ref, g2_ref, wo_ref, w1_ref, w2_ref, fg_ref, o_ref, *,
                final_norm):
    ns, tm = x_ref.shape[0], x_ref.shape[1]
    stack = lambda parts: jnp.concatenate(parts, axis=0)
    a = (_dot(stack([attn_ref[s] for s in range(ns)]), wo_ref[:MIX_HALF, :])
         + _dot(stack([mb_ref[s] for s in range(ns)]), wo_ref[MIX_HALF:, :]))
    x1 = [x_ref[s] + mod_ref[s, 2:3, :] * a[s * tm:(s + 1) * tm] for s in range(ns)]
    h2 = stack([_modulate(x1[s], g2_ref[...], mod_ref[s, 3:4, :], mod_ref[s, 4:5, :])
                for s in range(ns)]).astype(BF16)
    acc = None
    for c in range(0, FF_DIM, FF_CHUNK):
        hc = jnp.maximum(_dot(h2, w1_ref[:, c:c + FF_CHUNK]), 0.0)
        part = _dot((hc * hc).astype(BF16), w2_ref[c:c + FF_CHUNK, :])
        acc = part if acc is None else acc + part
    for s in range(ns):
        out = x1[s] + mod_ref[s, 5:6, :] * acc[s * tm:(s + 1) * tm]
        if final_norm:
            out = _rms(out) * fg_ref[...]
        o_ref[s] = out


def _out_call(xs, attn, mixb, mod, layer, g2, wo, w1, w2, fg, n_ctx_rows, tile0, final_norm, name):
    b, t, d = xs.shape
    ns = OUT_SAMPLES
    nt = t // TM - tile0
    ctx_tiles = n_ctx_rows // TM
    full = lambda a: pl.BlockSpec(a.shape, lambda i, j: (0,) * a.ndim)
    whole = pl.BlockSpec(memory_space=pltpu.VMEM)
    half_spec = pl.BlockSpec((ns, TM, MIX_HALF), lambda i, j: (i, j, 0))
    mod_spec = pl.BlockSpec((None, ns, N_MOD, d),
                            lambda i, j: (layer, jnp.where(j + tile0 < ctx_tiles, b // ns, i), 0, 0))
    return pl.pallas_call(
        functools.partial(_out_kernel, final_norm=final_norm),
        out_shape=jax.ShapeDtypeStruct((b, nt * TM, d), F32),
        grid=(b // ns, nt),
        in_specs=[pl.BlockSpec((ns, TM, d), lambda i, j: (i, j + tile0, 0)),
                  half_spec, half_spec, mod_spec, full(g2), whole, whole, whole, full(fg)],
        out_specs=pl.BlockSpec((ns, TM, d), lambda i, j: (i, j, 0)),
        compiler_params=pltpu.CompilerParams(
            dimension_semantics=("arbitrary", "arbitrary"), vmem_limit_bytes=VMEM_LIMIT),
        name=name,
    )(xs, attn, mixb, mod, g2, wo, w1, w2, fg)


def _axial_angles(length, d_rot):
    rows = length // GRID_W
    row = np.repeat(np.arange(rows), GRID_W).astype(np.float64)
    col = np.tile(np.arange(GRID_W), rows).astype(np.float64)
    d_axis = d_rot // 2
    inv = ROPE_THETA ** (-np.arange(0, d_axis, 2, dtype=np.float64) / d_axis)
    return np.concatenate([row[:, None] * inv, col[:, None] * inv], axis=-1)


def _rope_tables(seq, n_ctx, d_rot, lane0, period):
    ang = _axial_angles(seq, d_rot)
    cos = np.repeat(np.cos(ang), 2, axis=-1)
    sin = np.repeat(np.sin(ang), 2, axis=-1) * np.tile(np.array([-1.0, 1.0]), d_rot // 2)
    cos_p = np.ones((seq, period))
    sin_p = np.zeros((seq, period))
    cos_p[:, lane0:lane0 + d_rot] = cos
    sin_p[:, lane0:lane0 + d_rot] = sin
    cos_p = np.tile(cos_p, (1, LANES // period))
    sin_p = np.tile(sin_p, (1, LANES // period))
    cos_t = np.concatenate([np.ones((n_ctx, LANES)), cos_p], axis=0)
    sin_t = np.concatenate([np.zeros((n_ctx, LANES)), sin_p], axis=0)
    return jnp.asarray(cos_t, F32), jnp.asarray(sin_t, F32)


def _even_weights(w_in, q_g, k_g, sgu_g, sgu_w, sgu_b):
    d = w_in.shape[0]
    ev_q = A_Q_HEADS * A_HEAD_DIM
    ev_kv = A_KV_HEADS * A_HEAD_DIM
    wq = w_in[:, :ev_q].reshape(d, A_Q_HEADS, A_HEAD_DIM)
    slabs = []
    gq = []
    zero = jnp.zeros((d, A_HEAD_DIM), w_in.dtype)
    zg = jnp.zeros((A_HEAD_DIM,), q_g.dtype)
    for h in range(A_Q_HEADS):
        if h // A_GROUP == 0:
            slabs += [wq[:, h], zero]
            gq += [q_g, zg]
        else:
            slabs += [zero, wq[:, h]]
            gq += [zg, q_g]
    w = jnp.concatenate(slabs + [w_in[:, ev_q:]], axis=1).astype(BF16)
    gq = jnp.concatenate(gq)[None, :]
    gk = jnp.tile(k_g, A_KV_HEADS)[None, :]
    sgug = sgu_g.reshape(1, -1)
    sgub = jnp.repeat(sgu_b.T, B_GROUP_DIM, axis=1)
    return w, gq, gk, sgug, sgu_w.astype(BF16), sgub


def _odd_weights(w_in, w_uq, w_ukv):
    d = w_in.shape[0]
    c0 = C_Q_RANK + C_KV_RANK
    w_kr = w_in[:, c0:c0 + C_ROPE]
    kr_slab = jnp.zeros((d, LANES), w_in.dtype).at[:, C_NOPE:C_NOPE + C_ROPE].set(w_kr)
    w = jnp.concatenate([w_in[:, :c0], kr_slab, w_in[:, c0 + C_ROPE:]], axis=1).astype(BF16)
    uq = w_uq.reshape(C_Q_RANK, C_HEADS, C_NOPE + C_ROPE)
    uq = jnp.pad(uq, ((0, 0), (0, 0), (0, LANES - C_NOPE - C_ROPE))).reshape(C_Q_RANK, C_HEADS * LANES)
    ukv = w_ukv.reshape(C_KV_RANK, C_HEADS, C_NOPE + C_V)
    uk = jnp.pad(ukv[:, :, :C_NOPE], ((0, 0), (0, 0), (0, LANES - C_NOPE))).reshape(C_KV_RANK, -1)
    uv = jnp.pad(ukv[:, :, C_NOPE:], ((0, 0), (0, 0), (0, LANES - C_V))).reshape(C_KV_RANK, -1)
    return w, uq.astype(BF16), jnp.concatenate([uk, uv], axis=1).astype(BF16)


def _group_matrix(group_lanes):
    idx = np.arange(2 * LANES) // group_lanes
    return jnp.asarray((idx[:, None] == idx[None, :]).astype(np.float32) / A_HEAD_DIM, BF16)


def kernel(x, c, ctx, c_ctx, ada_w, ada_b, norm1_g, norm2_g, w_out, mlp_w1, mlp_w2, ev_w_in, ev_q_norm_g, ev_k_norm_g, ev_sgu_norm_g, ev_sgu_w, ev_sgu_b, od_w_in, od_q_norm_g, od_kv_norm_g, od_w_uq, od_w_ukv, od_conv_w, od_conv_b, od_ln_g, od_ln_b, final_g):
    b, seq, d = x.shape
    n_ctx = ctx.shape[1]
    depth = ada_w.shape[0]
    assert d == D_MODEL and seq % TM == 0 and n_ctx % TM == 0 and depth == 2
    ctx_tiles = n_ctx // TM

    xs = jnp.concatenate([ctx, x], axis=1)
    assert b % OUT_SAMPLES == 0
    r = -(-(b + OUT_SAMPLES) // SUBLANES) * SUBLANES
    cvec = jnp.concatenate([c] + [c_ctx[None, :]] * OUT_SAMPLES
                           + [jnp.zeros((r - b - OUT_SAMPLES, d), F32)], axis=0)
    mod = _ada_call(cvec, ada_w, ada_b).reshape(depth, r, N_MOD, d)
    fg = final_g[None, :]

    wo = w_out.astype(BF16)
    w1 = mlp_w1.astype(BF16)
    w2 = mlp_w2.astype(BF16)

    cos_e, sin_e = _rope_tables(seq, n_ctx, A_HEAD_DIM, 0, A_HEAD_DIM)
    w_e, gq, gk, sgug, sguw, sgub = _even_weights(ev_w_in[0], ev_q_norm_g[0], ev_k_norm_g[0],
                                                  ev_sgu_norm_g[0], ev_sgu_w[0], ev_sgu_b[0])
    q, k, v, mixb = _even_in_call(xs, mod, 0, norm1_g[0][None, :], w_e, gq, gk, cos_e, sin_e,
                                  _group_matrix(LANES), _group_matrix(A_HEAD_DIM), sgug, sguw, sgub,
                                  n_ctx)
    groups = tuple(((h,), 0, h // A_GROUP) for h in range(A_Q_HEADS))
    attn = _attn_call(q, k, v, n_ctx, 0, groups, "even_attention")
    xs = _out_call(xs, attn, mixb, mod, 0, norm2_g[0][None, :], wo[0], w1[0], w2[0], fg, n_ctx, 0,
                   False, "even_out_mlp")

    cos_o, sin_o = _rope_tables(seq, n_ctx, C_ROPE, C_NOPE, LANES)
    w_o, uq, ukv = _odd_weights(od_w_in[0], od_w_uq[0], od_w_ukv[0])
    q, k, v, y = _odd_in_call(xs, mod, 1, norm1_g[1][None, :], w_o, od_q_norm_g[0][None, :],
                              od_kv_norm_g[0][None, :], uq, ukv, cos_o, sin_o, n_ctx)
    groups = tuple(((h,), h, h) for h in range(C_HEADS))
    cw = jnp.broadcast_to(od_conv_w[0][:, None, :], (D_CONV, SUBLANES, MIX_HALF))
    conv = (y, cw, od_conv_b[0][None, :], od_ln_g[0][None, :], od_ln_b[0][None, :])
    attn, mixb = _attn_call(q, k, v, n_ctx, ctx_tiles, groups, "odd_attention", conv=conv)
    return _out_call(xs, attn, mixb, mod, 1, norm2_g[1][None, :], wo[1], w1[1], w2[1], fg, n_ctx,
                     ctx_tiles, True, "odd_out_mlp")
```

```python
import functools

import numpy as np
import jax
import jax.numpy as jnp
from jax import lax
from jax.experimental import pallas as pl
from jax.experimental.pallas import tpu as pltpu

F32 = jnp.float32
BF16 = jnp.bfloat16

D_MODEL = 1024
GRID_W = 64
ROPE_THETA = 10000.0
EPS = 1e-6
MIX_HALF = D_MODEL // 2
N_MOD = 6

A_HEAD_DIM = 64
A_Q_HEADS = 8
A_KV_HEADS = 2
A_GROUP = A_Q_HEADS // A_KV_HEADS
B_GROUPS = 8
B_GROUP_DIM = 64
B_CHUNK = 128
C_HEADS = 8
C_NOPE = 64
C_ROPE = 32
C_V = 64
C_Q_RANK = 256
C_KV_RANK = 128
D_CONV = 31
FF_DIM = 4 * D_MODEL

LANES = 128
TM = 256
FF_CHUNK = 1024
SUBLANES = 8
ATT_TILES = 2
OUT_SAMPLES = 2
CONV_ROWS = 32
HALO = 16
VMEM_LIMIT = 56 * 1024 * 1024
LOG2E = 1.4426950408889634


def _dot(a, b):
    return jnp.dot(a, b, preferred_element_type=F32)


def _dot_nt(a, b):
    return lax.dot_general(a, b, (((1,), (1,)), ((), ())), preferred_element_type=F32)


def _rms(x):
    return x * lax.rsqrt(jnp.mean(x * x, axis=-1, keepdims=True) + EPS)


def _modulate(x, g, shift, scale):
    return (_rms(x) * g) * (1.0 + scale) + shift


def _pair_swap(x):
    even = lax.broadcasted_iota(jnp.int32, (x.shape[0], LANES), 1) % 2 == 0
    slabs = []
    for s in range(0, x.shape[-1], LANES):
        xs = x[:, s:s + LANES]
        slabs.append(jnp.where(even, pltpu.roll(xs, LANES - 1, 1), pltpu.roll(xs, 1, 1)))
    return slabs[0] if len(slabs) == 1 else jnp.concatenate(slabs, axis=-1)


def _rope(x, cos, sin_signed):
    reps = x.shape[-1] // LANES
    if reps > 1:
        cos = jnp.concatenate([cos] * reps, axis=-1)
        sin_signed = jnp.concatenate([sin_signed] * reps, axis=-1)
    return x * cos + _pair_swap(x) * sin_signed


def _q_tile_index(ctx_tiles, nt):
    return lambda i, j: (i, jnp.where(j < ctx_tiles, j + nt - ctx_tiles, j - ctx_tiles), 0)


def _group_mean_sq(x, gmat):
    w = gmat.shape[0]
    x2 = (x * x).astype(BF16)
    parts = [_dot(x2[:, i:i + w], gmat) for i in range(0, x.shape[-1], w)]
    return parts[0] if len(parts) == 1 else jnp.concatenate(parts, axis=-1)


def _ada_kernel(c_ref, w_ref, b_ref, o_ref):
    c = c_ref[...]
    s = c * (1.0 / (1.0 + jnp.exp(-c)))
    o_ref[...] = _dot(s.astype(BF16), w_ref[...].astype(BF16)) + b_ref[...]


def _ada_call(cvec, ada_w, ada_b):
    depth, d, n = ada_w.shape
    r = cvec.shape[0]
    bn = 1536
    return pl.pallas_call(
        _ada_kernel,
        out_shape=jax.ShapeDtypeStruct((depth, r, n), F32),
        grid=(depth, n // bn),
        in_specs=[
            pl.BlockSpec((r, d), lambda i, j: (0, 0)),
            pl.BlockSpec((None, d, bn), lambda i, j: (i, 0, j)),
            pl.BlockSpec((None, 1, bn), lambda i, j: (i, 0, j)),
        ],
        out_specs=pl.BlockSpec((None, r, bn), lambda i, j: (i, 0, j)),
        compiler_params=pltpu.CompilerParams(
            dimension_semantics=("arbitrary", "arbitrary"), vmem_limit_bytes=VMEM_LIMIT),
        name="ada_mod",
    )(cvec, ada_w, ada_b.reshape(depth, 1, n))


def _even_in_kernel(ctx_ref, x_ref, mod_ref, g1_ref, w_ref, gq_ref, gk_ref, cos_ref, sin_ref,
                    gmat_ref, gmat64_ref, sgug_ref, sguw_ref, sgub_ref, q_ref, k_ref, v_ref, mb_ref,
                    *, ctx_tiles):
    x = jnp.where(pl.program_id(1) < ctx_tiles, ctx_ref[...], x_ref[...])
    h = _modulate(x, g1_ref[...], mod_ref[0:1, :], mod_ref[1:2, :])
    p = _dot(h.astype(BF16), w_ref[...])
    nq = A_Q_HEADS * LANES
    q = p[:, :nq]
    k = p[:, nq:nq + LANES]
    v = p[:, nq + LANES:nq + 2 * LANES]
    z = p[:, nq + 2 * LANES:]
    gmat = gmat_ref[...]
    gmat64 = gmat64_ref[...]
    cos = cos_ref[...]
    sin = sin_ref[...]

    qn = q * lax.rsqrt(_group_mean_sq(q, gmat) + EPS) * gq_ref[...]
    q_ref[...] = (_rope(qn, cos, sin) * (A_HEAD_DIM ** -0.5 * LOG2E)).astype(BF16)
    kn = k * lax.rsqrt(_group_mean_sq(k, gmat64[:LANES, :LANES]) + EPS) * gk_ref[...]
    k_ref[...] = _rope(kn, cos, sin).astype(BF16)
    low = lax.broadcasted_iota(jnp.int32, v.shape, 1) < A_HEAD_DIM
    v_ref[:, :LANES] = jnp.where(low, v, 1.0).astype(BF16)
    v_ref[:, LANES:] = jnp.where(low, pltpu.roll(v, A_HEAD_DIM, 1), 1.0).astype(BF16)

    ge = 0.5 * z * (1.0 + jnp.tanh(np.sqrt(2.0 / np.pi).astype(np.float32)
                                   * (z + 0.044715 * (z * z * z))))
    u = ge[:, :MIX_HALF]
    vv = ge[:, MIX_HALF:]
    vn = vv * lax.rsqrt(_group_mean_sq(vv, gmat64) + EPS) * sgug_ref[...]
    vnb = vn.astype(BF16)
    lane = lax.broadcasted_iota(jnp.int32, (B_CHUNK, LANES), 1)
    low = lane < B_GROUP_DIM
    rows = []
    for c in range(x.shape[0] // B_CHUNK):
        slabs = []
        for s in range(MIX_HALF // LANES):
            vs = vnb[c * B_CHUNK:(c + 1) * B_CHUNK, s * LANES:(s + 1) * LANES]
            r0 = _dot(sguw_ref[2 * s], vs)
            r1 = _dot(sguw_ref[2 * s + 1], vs)
            slabs.append(jnp.where(low, r0, r1))
        rows.append(jnp.concatenate(slabs, axis=-1) + sgub_ref[...])
    sv = jnp.concatenate(rows, axis=0)
    mb_ref[...] = (u * sv).astype(BF16)


def _even_in_call(ctx, x, mod, layer, g1, w_in, gq, gk, cos, sin, gmat, gmat64, sgug, sguw, sgub):
    b, seq, d = x.shape
    n_ctx_rows = ctx.shape[1]
    ctx_tiles = n_ctx_rows // TM
    t = n_ctx_rows + seq
    nt = t // TM
    nq = A_Q_HEADS * LANES
    full = lambda a: pl.BlockSpec(a.shape, lambda i, j: (0,) * a.ndim)
    return pl.pallas_call(
        functools.partial(_even_in_kernel, ctx_tiles=ctx_tiles),
        out_shape=(
            jax.ShapeDtypeStruct((b, t, nq), BF16),
            jax.ShapeDtypeStruct((b, t, LANES), BF16),
            jax.ShapeDtypeStruct((b, t, A_KV_HEADS * LANES), BF16),
            jax.ShapeDtypeStruct((b, t, MIX_HALF), BF16),
        ),
        grid=(b, nt),
        in_specs=[
            pl.BlockSpec((None, TM, d), lambda i, j: (i, jnp.minimum(j, ctx_tiles - 1), 0)),
            pl.BlockSpec((None, TM, d), lambda i, j: (i, jnp.maximum(j - ctx_tiles, 0), 0)),
            pl.BlockSpec((None, None, N_MOD, d),
                         lambda i, j: (layer, jnp.where(j < ctx_tiles, b, i), 0, 0)),
            full(g1),
            pl.BlockSpec(memory_space=pltpu.VMEM),
            full(gq), full(gk),
            pl.BlockSpec((TM, LANES), lambda i, j: (j, 0)),
            pl.BlockSpec((TM, LANES), lambda i, j: (j, 0)),
            full(gmat), full(gmat64), full(sgug), full(sguw), full(sgub),
        ],
        out_specs=(
            pl.BlockSpec((None, TM, nq), _q_tile_index(ctx_tiles, nt)),
            pl.BlockSpec((None, TM, LANES), lambda i, j: (i, j, 0)),
            pl.BlockSpec((None, TM, A_KV_HEADS * LANES), lambda i, j: (i, j, 0)),
            pl.BlockSpec((None, TM, MIX_HALF), lambda i, j: (i, j, 0)),
        ),
        compiler_params=pltpu.CompilerParams(
            dimension_semantics=("arbitrary", "arbitrary"), vmem_limit_bytes=VMEM_LIMIT),
        name="even_in_proj",
    )(ctx, x, mod, g1, w_in, gq, gk, cos, sin, gmat, gmat64, sgug, sguw, sgub)


def _attn_body(q_ref, k_ref, v_ref, o_ref, n_keys, groups):
    for r0 in range(0, q_ref.shape[0], TM):
        _attn_rows(q_ref.at[r0:r0 + TM, :], k_ref, v_ref, o_ref.at[r0:r0 + TM, :], n_keys, groups)


def _attn_rows(q_ref, k_ref, v_ref, o_ref, n_keys, groups):
    tm = q_ref.shape[0]
    half = LANES // 2
    low = lax.broadcasted_iota(jnp.int32, (tm, LANES), 1) < half
    res = {}
    for heads, ks, vs in groups:
        q = [q_ref[:, h * LANES:(h + 1) * LANES] for h in heads]
        q = q[0] if len(q) == 1 else jnp.concatenate(q, axis=0)
        s = _dot_nt(q, k_ref[:n_keys, ks * LANES:(ks + 1) * LANES])
        p = jnp.exp2(s - jnp.max(s, axis=-1, keepdims=True))
        vp = vs // 2 * 2
        pv = _dot(p.astype(BF16), v_ref[:n_keys, vp * LANES:(vp + 2) * LANES])
        pv = pv[:, (vs - vp) * LANES:(vs - vp + 1) * LANES]
        sw = pltpu.roll(pv, half, 1)
        for i, h in enumerate(heads):
            a, b = pv[i * tm:(i + 1) * tm], sw[i * tm:(i + 1) * tm]
            res[h] = a / b if h % 2 == 0 else b / a
    for pair in range(len(res) // 2):
        o_ref[:, pair * LANES:(pair + 1) * LANES] = jnp.where(
            low, res[2 * pair], res[2 * pair + 1]).astype(BF16)


def _conv_prepare(yp, yc, yn, has_prev, has_next, ybuf, ysh):
    tm = yc.shape[0]
    ybuf[0:HALO, :] = jnp.where(has_prev, yp, 0.0)
    ybuf[HALO:HALO + tm, :] = yc
    ybuf[HALO + tm:, :] = jnp.where(has_next, yn, 0.0)
    n_sh = ysh.shape[1]
    for s in range(1, SUBLANES):
        ysh[s - 1] = ybuf[s:s + n_sh, :]


def _conv_ln_silu_rows(r0, ybuf, ysh, cw_ref, cb_ref, lng_ref, lnb_ref):
    base = HALO - D_CONV // 2
    nch = ybuf.shape[1]
    acc = None
    for tap in range(D_CONV):
        a, s = divmod(base + tap, SUBLANES)
        lo = SUBLANES * a + r0
        win = ybuf[lo:lo + CONV_ROWS, :] if s == 0 else ysh[s - 1, lo:lo + CONV_ROWS, :]
        term = win.reshape(CONV_ROWS // SUBLANES, SUBLANES, nch) * cw_ref[tap]
        acc = term if acc is None else acc + term
    y = acc.reshape(CONV_ROWS, nch) + cb_ref[...]
    mu = jnp.mean(y, axis=-1, keepdims=True)
    yc = y - mu
    var = jnp.mean(yc * yc, axis=-1, keepdims=True)
    ln = yc * lax.rsqrt(var + EPS) * lng_ref[...] + lnb_ref[...]
    return (ln * (1.0 / (1.0 + jnp.exp(-ln)))).astype(BF16)


def _attn_kernel(q_ref, k_ref, v_ref, o_ref, *, groups):
    _attn_body(q_ref, k_ref, v_ref, o_ref, k_ref.shape[0], groups)


def _attn_call(q, k, v, seq, groups, name):
    b, t, nq = q.shape
    rows = ATT_TILES * TM
    return pl.pallas_call(
        functools.partial(_attn_kernel, groups=groups),
        out_shape=jax.ShapeDtypeStruct((b, seq, MIX_HALF), BF16),
        grid=(b, seq // rows),
        in_specs=[
            pl.BlockSpec((None, rows, nq), lambda i, j: (i, j, 0)),
            pl.BlockSpec((None, t, k.shape[-1]), lambda i, j: (i, 0, 0)),
            pl.BlockSpec((None, t, v.shape[-1]), lambda i, j: (i, 0, 0)),
        ],
        out_specs=pl.BlockSpec((None, rows, MIX_HALF), lambda i, j: (i, j, 0)),
        compiler_params=pltpu.CompilerParams(
            dimension_semantics=("arbitrary", "arbitrary"), vmem_limit_bytes=VMEM_LIMIT),
        name=name,
    )(q, k, v)


def _ctx_attn_call(q, k, v, seq, n_ctx, groups, name):
    b, t, nq = q.shape
    assert seq % n_ctx == 0
    return pl.pallas_call(
        functools.partial(_attn_kernel, groups=groups),
        out_shape=jax.ShapeDtypeStruct((b, n_ctx, MIX_HALF), BF16),
        grid=(b,),
        in_specs=[
            pl.BlockSpec((None, n_ctx, nq), lambda i: (i, seq // n_ctx, 0)),
            pl.BlockSpec((None, n_ctx, k.shape[-1]), lambda i: (i, 0, 0)),
            pl.BlockSpec((None, n_ctx, v.shape[-1]), lambda i: (i, 0, 0)),
        ],
        out_specs=pl.BlockSpec((None, n_ctx, MIX_HALF), lambda i: (i, 0, 0)),
        compiler_params=pltpu.CompilerParams(
            dimension_semantics=("arbitrary",), vmem_limit_bytes=VMEM_LIMIT),
        name=name,
    )(q, k, v)


def _odd_in_kernel(x_ref, mod_ref, g1_ref, w_ref, gcq_ref, gckv_ref, wuq_ref, wukv_ref,
                   cos_ref, sin_ref, q_ref, k_ref, v_ref, y_ref):
    x = x_ref[...]
    h = _modulate(x, g1_ref[...], mod_ref[0:1, :], mod_ref[1:2, :])
    p = _dot(h.astype(BF16), w_ref[...])
    cq = p[:, :C_Q_RANK]
    ckv = p[:, C_Q_RANK:C_Q_RANK + C_KV_RANK]
    o = C_Q_RANK + C_KV_RANK
    kr = p[:, o:o + LANES]
    z = p[:, o + LANES:]
    cos = cos_ref[...]
    sin = sin_ref[...]

    q = _dot((_rms(cq) * gcq_ref[...]).astype(BF16), wuq_ref[...])
    q_ref[...] = (_rope(q, cos, sin) * ((C_NOPE + C_ROPE) ** -0.5 * LOG2E)).astype(BF16)
    kv = _dot((_rms(ckv) * gckv_ref[...]).astype(BF16), wukv_ref[...])
    nk = C_HEADS * LANES
    krr = _rope(kr, cos, sin)
    k_ref[...] = (kv[:, :nk] + jnp.concatenate([krr] * C_HEADS, axis=-1)).astype(BF16)
    v = kv[:, nk:]
    high = lax.broadcasted_iota(jnp.int32, v.shape, 1) % LANES >= C_V
    v_ref[...] = jnp.where(high, 1.0, v).astype(BF16)
    a = z[:, :MIX_HALF]
    gt = z[:, MIX_HALF:]
    y_ref[...] = a * (1.0 / (1.0 + jnp.exp(-gt)))


def _odd_in_call(xs, mod, layer, g1, w_in, gcq, gckv, wuq, wukv, cos, sin, n_ctx_rows):
    b, t, d = xs.shape
    nt = t // TM
    nk = C_HEADS * LANES
    full = lambda a: pl.BlockSpec(a.shape, lambda i, j: (0,) * a.ndim)
    return pl.pallas_call(
        _odd_in_kernel,
        out_shape=(
            jax.ShapeDtypeStruct((b, t, nk), BF16),
            jax.ShapeDtypeStruct((b, t, nk), BF16),
            jax.ShapeDtypeStruct((b, t, nk), BF16),
            jax.ShapeDtypeStruct((b, t, MIX_HALF), F32),
        ),
        grid=(b, nt),
        in_specs=[
            pl.BlockSpec((None, TM, d), lambda i, j: (i, j, 0)),
            pl.BlockSpec((None, None, N_MOD, d),
                         lambda i, j: (layer, jnp.where(j < n_ctx_rows // TM, b, i), 0, 0)),
            full(g1),
            pl.BlockSpec(memory_space=pltpu.VMEM),
            full(gcq), full(gckv),
            pl.BlockSpec(memory_space=pltpu.VMEM),
            pl.BlockSpec(memory_space=pltpu.VMEM),
            pl.BlockSpec((TM, LANES), lambda i, j: (j, 0)),
            pl.BlockSpec((TM, LANES), lambda i, j: (j, 0)),
        ],
        out_specs=(
            pl.BlockSpec((None, TM, nk), _q_tile_index(n_ctx_rows // TM, nt)),
            pl.BlockSpec((None, TM, nk), lambda i, j: (i, j, 0)),
            pl.BlockSpec((None, TM, nk), lambda i, j: (i, j, 0)),
            pl.BlockSpec((None, TM, MIX_HALF), lambda i, j: (i, j, 0)),
        ),
        compiler_params=pltpu.CompilerParams(
            dimension_semantics=("arbitrary", "arbitrary"), vmem_limit_bytes=VMEM_LIMIT),
        name="odd_in_proj",
    )(xs, mod, g1, w_in, gcq, gckv, wuq, wukv, cos, sin)


def _mlp(h2, w1_ref, w2_ref):
    acc = None
    for c in range(0, FF_DIM, FF_CHUNK):
        hc = jnp.maximum(_dot(h2, w1_ref[:, c:c + FF_CHUNK]), 0.0)
        part = _dot((hc * hc).astype(BF16), w2_ref[c:c + FF_CHUNK, :])
        acc = part if acc is None else acc + part
    return acc


def _even_out_kernel(ctx_ref, x_ref, actx_ref, alat_ref, mb_ref, mod_ref, g2_ref, wo_ref, w1_ref,
                     w2_ref, o_ref, *, ctx_tiles):
    ns, tm = x_ref.shape[0], x_ref.shape[1]
    is_ctx = pl.program_id(1) < ctx_tiles
    stack = lambda parts: jnp.concatenate(parts, axis=0)
    attn = jnp.where(is_ctx, stack([actx_ref[s] for s in range(ns)]),
                     stack([alat_ref[s] for s in range(ns)]))
    a = (_dot(attn, wo_ref[:MIX_HALF, :])
         + _dot(stack([mb_ref[s] for s in range(ns)]), wo_ref[MIX_HALF:, :]))
    x1 = [jnp.where(is_ctx, ctx_ref[s], x_ref[s]) + mod_ref[s, 2:3, :] * a[s * tm:(s + 1) * tm]
          for s in range(ns)]
    h2 = stack([_modulate(x1[s], g2_ref[...], mod_ref[s, 3:4, :], mod_ref[s, 4:5, :])
                for s in range(ns)]).astype(BF16)
    acc = _mlp(h2, w1_ref, w2_ref)
    for s in range(ns):
        o_ref[s] = x1[s] + mod_ref[s, 5:6, :] * acc[s * tm:(s + 1) * tm]


def _odd_out_kernel(x_ref, attn_ref, yp_ref, yc_ref, yn_ref, mod_ref, g2_ref, wo_ref, w1_ref, w2_ref,
                    cw_ref, cb_ref, lng_ref, lnb_ref, fg_ref, o_ref, ybuf, ysh, *,
                    tile0, ctx_tiles, last_tile, final_norm):
    j = pl.program_id(1) + tile0
    has_prev = jnp.logical_and(j > 0, j != ctx_tiles)
    has_next = jnp.logical_and(j < last_tile, j != ctx_tiles - 1)
    ns, tm = x_ref.shape[0], x_ref.shape[1]
    for s in range(ns):
        _conv_prepare(yp_ref[s], yc_ref[s], yn_ref[s], has_prev, has_next, ybuf.at[s], ysh.at[s])
        mixb = jnp.concatenate(
            [_conv_ln_silu_rows(r0, ybuf.at[s], ysh.at[s], cw_ref, cb_ref, lng_ref, lnb_ref)
             for r0 in range(0, tm, CONV_ROWS)], axis=0)
        a = _dot(attn_ref[s], wo_ref[:MIX_HALF, :]) + _dot(mixb, wo_ref[MIX_HALF:, :])
        x1 = x_ref[s] + mod_ref[s, 2:3, :] * a
        h2 = _modulate(x1, g2_ref[...], mod_ref[s, 3:4, :], mod_ref[s, 4:5, :]).astype(BF16)
        out = x1 + mod_ref[s, 5:6, :] * _mlp(h2, w1_ref, w2_ref)
        if final_norm:
            out = _rms(out) * fg_ref[...]
        o_ref[s] = out


def _out_specs(b, d, layer, ctx_tiles, tile0):
    ns = OUT_SAMPLES
    half_spec = pl.BlockSpec((ns, TM, MIX_HALF), lambda i, j: (i, j, 0))
    mod_spec = pl.BlockSpec((None, ns, N_MOD, d),
                            lambda i, j: (layer, jnp.where(j + tile0 < ctx_tiles, b // ns, i), 0, 0))
    out_spec = pl.BlockSpec((ns, TM, d), lambda i, j: (i, j, 0))
    params = pltpu.CompilerParams(
        dimension_semantics=("arbitrary", "arbitrary"), vmem_limit_bytes=VMEM_LIMIT)
    return half_spec, mod_spec, out_spec, params


def _even_out_call(ctx, x, attn_ctx, attn_lat, mixb, mod, layer, g2, wo, w1, w2):
    b, seq, d = x.shape
    ns = OUT_SAMPLES
    ctx_tiles = ctx.shape[1] // TM
    nt = ctx_tiles + seq // TM
    full = lambda a: pl.BlockSpec(a.shape, lambda i, j: (0,) * a.ndim)
    whole = pl.BlockSpec(memory_space=pltpu.VMEM)
    half_spec, mod_spec, out_spec, params = _out_specs(b, d, layer, ctx_tiles, 0)
    return pl.pallas_call(
        functools.partial(_even_out_kernel, ctx_tiles=ctx_tiles),
        out_shape=jax.ShapeDtypeStruct((b, nt * TM, d), F32),
        grid=(b // ns, nt),
        in_specs=[pl.BlockSpec((ns, TM, d), lambda i, j: (i, jnp.minimum(j, ctx_tiles - 1), 0)),
                  pl.BlockSpec((ns, TM, d), lambda i, j: (i, jnp.maximum(j - ctx_tiles, 0), 0)),
                  pl.BlockSpec((ns, TM, MIX_HALF),
                               lambda i, j: (i, jnp.minimum(j, ctx_tiles - 1), 0)),
                  pl.BlockSpec((ns, TM, MIX_HALF),
                               lambda i, j: (i, jnp.maximum(j - ctx_tiles, 0), 0)),
                  half_spec, mod_spec, full(g2), whole, whole, whole],
        out_specs=out_spec, compiler_params=params, name="even_out_mlp",
    )(ctx, x, attn_ctx, attn_lat, mixb, mod, g2, wo, w1, w2)


def _odd_out_call(xs, attn, y, mod, layer, g2, wo, w1, w2, cw, cb, lng, lnb, fg, n_ctx_rows, tile0,
                  final_norm):
    b, t, d = xs.shape
    ns = OUT_SAMPLES
    nt = t // TM - tile0
    ctx_tiles = n_ctx_rows // TM
    full = lambda a: pl.BlockSpec(a.shape, lambda i, j: (0,) * a.ndim)
    whole = pl.BlockSpec(memory_space=pltpu.VMEM)
    half_spec, mod_spec, out_spec, params = _out_specs(b, d, layer, ctx_tiles, tile0)
    hb = TM // HALO
    last_halo = t // HALO - 1
    kern = functools.partial(_odd_out_kernel, tile0=tile0, ctx_tiles=ctx_tiles,
                             last_tile=t // TM - 1, final_norm=final_norm)
    return pl.pallas_call(
        kern,
        out_shape=jax.ShapeDtypeStruct((b, nt * TM, d), F32),
        grid=(b // ns, nt),
        in_specs=[pl.BlockSpec((ns, TM, d), lambda i, j: (i, j + tile0, 0)),
                  half_spec,
                  pl.BlockSpec((ns, HALO, MIX_HALF),
                               lambda i, j: (i, jnp.maximum((j + tile0) * hb - 1, 0), 0)),
                  pl.BlockSpec((ns, TM, MIX_HALF), lambda i, j: (i, j + tile0, 0)),
                  pl.BlockSpec((ns, HALO, MIX_HALF),
                               lambda i, j: (i, jnp.minimum((j + tile0 + 1) * hb, last_halo), 0)),
                  mod_spec, full(g2), whole, whole, whole,
                  full(cw), full(cb), full(lng), full(lnb), full(fg)],
        out_specs=out_spec,
        scratch_shapes=[pltpu.VMEM((ns, TM + 2 * HALO, MIX_HALF), F32),
                        pltpu.VMEM((ns, SUBLANES - 1, TM + 2 * HALO - SUBLANES, MIX_HALF), F32)],
        compiler_params=params, name="odd_out_mlp",
    )(xs, attn, y, y, y, mod, g2, wo, w1, w2, cw, cb, lng, lnb, fg)


def _axial_angles(length, d_rot):
    rows = length // GRID_W
    row = np.repeat(np.arange(rows), GRID_W).astype(np.float64)
    col = np.tile(np.arange(GRID_W), rows).astype(np.float64)
    d_axis = d_rot // 2
    inv = ROPE_THETA ** (-np.arange(0, d_axis, 2, dtype=np.float64) / d_axis)
    return np.concatenate([row[:, None] * inv, col[:, None] * inv], axis=-1)


def _rope_tables(seq, n_ctx, d_rot, lane0, period):
    ang = _axial_angles(seq, d_rot)
    cos = np.repeat(np.cos(ang), 2, axis=-1)
    sin = np.repeat(np.sin(ang), 2, axis=-1) * np.tile(np.array([-1.0, 1.0]), d_rot // 2)
    cos_p = np.ones((seq, period))
    sin_p = np.zeros((seq, period))
    cos_p[:, lane0:lane0 + d_rot] = cos
    sin_p[:, lane0:lane0 + d_rot] = sin
    cos_p = np.tile(cos_p, (1, LANES // period))
    sin_p = np.tile(sin_p, (1, LANES // period))
    cos_t = np.concatenate([np.ones((n_ctx, LANES)), cos_p], axis=0)
    sin_t = np.concatenate([np.zeros((n_ctx, LANES)), sin_p], axis=0)
    return jnp.asarray(cos_t, F32), jnp.asarray(sin_t, F32)


def _even_weights(w_in, q_g, k_g, sgu_g, sgu_w, sgu_b):
    d = w_in.shape[0]
    ev_q = A_Q_HEADS * A_HEAD_DIM
    ev_kv = A_KV_HEADS * A_HEAD_DIM
    wq = w_in[:, :ev_q].reshape(d, A_Q_HEADS, A_HEAD_DIM)
    slabs = []
    gq = []
    zero = jnp.zeros((d, A_HEAD_DIM), w_in.dtype)
    zg = jnp.zeros((A_HEAD_DIM,), q_g.dtype)
    for h in range(A_Q_HEADS):
        if h // A_GROUP == 0:
            slabs += [wq[:, h], zero]
            gq += [q_g, zg]
        else:
            slabs += [zero, wq[:, h]]
            gq += [zg, q_g]
    w = jnp.concatenate(slabs + [w_in[:, ev_q:]], axis=1).astype(BF16)
    gq = jnp.concatenate(gq)[None, :]
    gk = jnp.tile(k_g, A_KV_HEADS)[None, :]
    sgug = sgu_g.reshape(1, -1)
    sgub = jnp.repeat(sgu_b.T, B_GROUP_DIM, axis=1)
    return w, gq, gk, sgug, sgu_w.astype(BF16), sgub


def _odd_weights(w_in, w_uq, w_ukv):
    d = w_in.shape[0]
    c0 = C_Q_RANK + C_KV_RANK
    w_kr = w_in[:, c0:c0 + C_ROPE]
    kr_slab = jnp.zeros((d, LANES), w_in.dtype).at[:, C_NOPE:C_NOPE + C_ROPE].set(w_kr)
    w = jnp.concatenate([w_in[:, :c0], kr_slab, w_in[:, c0 + C_ROPE:]], axis=1).astype(BF16)
    uq = w_uq.reshape(C_Q_RANK, C_HEADS, C_NOPE + C_ROPE)
    uq = jnp.pad(uq, ((0, 0), (0, 0), (0, LANES - C_NOPE - C_ROPE))).reshape(C_Q_RANK, C_HEADS * LANES)
    ukv = w_ukv.reshape(C_KV_RANK, C_HEADS, C_NOPE + C_V)
    uk = jnp.pad(ukv[:, :, :C_NOPE], ((0, 0), (0, 0), (0, LANES - C_NOPE))).reshape(C_KV_RANK, -1)
    uv = jnp.pad(ukv[:, :, C_NOPE:], ((0, 0), (0, 0), (0, LANES - C_V))).reshape(C_KV_RANK, -1)
    return w, uq.astype(BF16), jnp.concatenate([uk, uv], axis=1).astype(BF16)


def _group_matrix(group_lanes):
    idx = np.arange(2 * LANES) // group_lanes
    return jnp.asarray((idx[:, None] == idx[None, :]).astype(np.float32) / A_HEAD_DIM, BF16)


def kernel(x, c, ctx, c_ctx, ada_w, ada_b, norm1_g, norm2_g, w_out, mlp_w1, mlp_w2, ev_w_in, ev_q_norm_g, ev_k_norm_g, ev_sgu_norm_g, ev_sgu_w, ev_sgu_b, od_w_in, od_q_norm_g, od_kv_norm_g, od_w_uq, od_w_ukv, od_conv_w, od_conv_b, od_ln_g, od_ln_b, final_g):
    b, seq, d = x.shape
    n_ctx = ctx.shape[1]
    depth = ada_w.shape[0]
    assert d == D_MODEL and seq % TM == 0 and n_ctx % TM == 0 and depth == 2
    ctx_tiles = n_ctx // TM

    assert b % OUT_SAMPLES == 0
    r = -(-(b + OUT_SAMPLES) // SUBLANES) * SUBLANES
    cvec = jnp.concatenate([c] + [c_ctx[None, :]] * OUT_SAMPLES
                           + [jnp.zeros((r - b - OUT_SAMPLES, d), F32)], axis=0)
    mod = _ada_call(cvec, ada_w, ada_b).reshape(depth, r, N_MOD, d)
    fg = final_g[None, :]

    wo = w_out.astype(BF16)
    w1 = mlp_w1.astype(BF16)
    w2 = mlp_w2.astype(BF16)

    cos_e, sin_e = _rope_tables(seq, n_ctx, A_HEAD_DIM, 0, A_HEAD_DIM)
    w_e, gq, gk, sgug, sguw, sgub = _even_weights(ev_w_in[0], ev_q_norm_g[0], ev_k_norm_g[0],
                                                  ev_sgu_norm_g[0], ev_sgu_w[0], ev_sgu_b[0])
    q, k, v, mixb = _even_in_call(ctx, x, mod, 0, norm1_g[0][None, :], w_e, gq, gk, cos_e, sin_e,
                                  _group_matrix(LANES), _group_matrix(A_HEAD_DIM), sgug, sguw, sgub)
    groups = tuple(((h,), 0, h // A_GROUP) for h in range(A_Q_HEADS))
    attn_lat = _attn_call(q, k, v, seq, groups, "even_attention")
    attn_ctx = _ctx_attn_call(q, k, v, seq, n_ctx, groups, "even_ctx_attention")
    xs = _even_out_call(ctx, x, attn_ctx, attn_lat, mixb, mod, 0, norm2_g[0][None, :], wo[0], w1[0],
                        w2[0])

    cos_o, sin_o = _rope_tables(seq, n_ctx, C_ROPE, C_NOPE, LANES)
    w_o, uq, ukv = _odd_weights(od_w_in[0], od_w_uq[0], od_w_ukv[0])
    q, k, v, y = _odd_in_call(xs, mod, 1, norm1_g[1][None, :], w_o, od_q_norm_g[0][None, :],
                              od_kv_norm_g[0][None, :], uq, ukv, cos_o, sin_o, n_ctx)
    groups = tuple(((h,), h, h) for h in range(C_HEADS))
    cw = jnp.broadcast_to(od_conv_w[0][:, None, :], (D_CONV, SUBLANES, MIX_HALF))
    attn = _attn_call(q, k, v, seq, groups, "odd_attention")
    return _odd_out_call(xs, attn, y, mod, 1, norm2_g[1][None, :], wo[1], w1[1], w2[1], cw,
                         od_conv_b[0][None, :], od_ln_g[0][None, :], od_ln_b[0][None, :], fg, n_ctx,
                         ctx_tiles, True)
```

```python
import functools

import numpy as np
import jax
import jax.numpy as jnp
from jax import lax
from jax.experimental import pallas as pl
from jax.experimental.pallas import tpu as pltpu

F32 = jnp.float32
BF16 = jnp.bfloat16

D_MODEL = 1024
GRID_W = 64
ROPE_THETA = 10000.0
EPS = 1e-6
MIX_HALF = D_MODEL // 2
N_MOD = 6

A_HEAD_DIM = 64
A_Q_HEADS = 8
A_KV_HEADS = 2
A_GROUP = A_Q_HEADS // A_KV_HEADS
B_GROUPS = 8
B_GROUP_DIM = 64
B_CHUNK = 128
C_HEADS = 8
C_NOPE = 64
C_ROPE = 32
C_V = 64
C_Q_RANK = 256
C_KV_RANK = 128
D_CONV = 31
FF_DIM = 4 * D_MODEL

LANES = 128
TM = 256
FF_CHUNK = 1024
SUBLANES = 8
ATT_TILES = 2
IN_SAMPLES = 4
OUT_SAMPLES = 2
CONV_ROWS = 32
HALO = 16
VMEM_LIMIT = 56 * 1024 * 1024
LOG2E = 1.4426950408889634


def _dot(a, b):
    return jnp.dot(a, b, preferred_element_type=F32)


def _dot_nt(a, b):
    return lax.dot_general(a, b, (((1,), (1,)), ((), ())), preferred_element_type=F32)


def _rms(x):
    return x * lax.rsqrt(jnp.mean(x * x, axis=-1, keepdims=True) + EPS)


def _modulate(x, g, shift, scale):
    return (_rms(x) * g) * (1.0 + scale) + shift


def _pair_swap(x):
    even = lax.broadcasted_iota(jnp.int32, (x.shape[0], LANES), 1) % 2 == 0
    slabs = []
    for s in range(0, x.shape[-1], LANES):
        xs = x[:, s:s + LANES]
        slabs.append(jnp.where(even, pltpu.roll(xs, LANES - 1, 1), pltpu.roll(xs, 1, 1)))
    return slabs[0] if len(slabs) == 1 else jnp.concatenate(slabs, axis=-1)


def _rope(x, cos, sin_signed):
    reps = x.shape[-1] // LANES
    if reps > 1:
        cos = jnp.concatenate([cos] * reps, axis=-1)
        sin_signed = jnp.concatenate([sin_signed] * reps, axis=-1)
    return x * cos + _pair_swap(x) * sin_signed


def _q_tile_index(ctx_tiles, nt):
    return lambda i, j: (i, jnp.where(j < ctx_tiles, j + nt - ctx_tiles, j - ctx_tiles), 0)


def _group_mean_sq(x, gmat):
    w = gmat.shape[0]
    x2 = (x * x).astype(BF16)
    parts = [_dot(x2[:, i:i + w], gmat) for i in range(0, x.shape[-1], w)]
    return parts[0] if len(parts) == 1 else jnp.concatenate(parts, axis=-1)


def _ada_kernel(c_ref, w_ref, b_ref, o_ref):
    c = c_ref[...]
    s = c * (1.0 / (1.0 + jnp.exp(-c)))
    o_ref[...] = _dot(s.astype(BF16), w_ref[...].astype(BF16)) + b_ref[...]


def _ada_call(cvec, ada_w, ada_b):
    depth, d, n = ada_w.shape
    r = cvec.shape[0]
    bn = 1536
    return pl.pallas_call(
        _ada_kernel,
        out_shape=jax.ShapeDtypeStruct((depth, r, n), F32),
        grid=(depth, n // bn),
        in_specs=[
            pl.BlockSpec((r, d), lambda i, j: (0, 0)),
            pl.BlockSpec((None, d, bn), lambda i, j: (i, 0, j)),
            pl.BlockSpec((None, 1, bn), lambda i, j: (i, 0, j)),
        ],
        out_specs=pl.BlockSpec((None, r, bn), lambda i, j: (i, 0, j)),
        compiler_params=pltpu.CompilerParams(
            dimension_semantics=("arbitrary", "arbitrary"), vmem_limit_bytes=VMEM_LIMIT),
        name="ada_mod",
    )(cvec, ada_w, ada_b.reshape(depth, 1, n))


def _even_in_kernel(ctx_ref, x_ref, mod_ref, g1_ref, w_ref, gq_ref, gk_ref, cos_ref, sin_ref,
                    gmat_ref, gmat64_ref, sgug_ref, sguw_ref, sgub_ref, q_ref, k_ref, v_ref, mb_ref,
                    *, ctx_tiles):
    for s in range(x_ref.shape[0]):
        _even_in_tile(ctx_ref.at[s], x_ref.at[s], mod_ref.at[s], g1_ref, w_ref, gq_ref, gk_ref,
                      cos_ref, sin_ref, gmat_ref, gmat64_ref, sgug_ref, sguw_ref, sgub_ref,
                      q_ref.at[s], k_ref.at[s], v_ref.at[s], mb_ref.at[s], ctx_tiles)


def _even_in_tile(ctx_ref, x_ref, mod_ref, g1_ref, w_ref, gq_ref, gk_ref, cos_ref, sin_ref,
                  gmat_ref, gmat64_ref, sgug_ref, sguw_ref, sgub_ref, q_ref, k_ref, v_ref, mb_ref,
                  ctx_tiles):
    x = jnp.where(pl.program_id(1) < ctx_tiles, ctx_ref[...], x_ref[...])
    h = _modulate(x, g1_ref[...], mod_ref[0:1, :], mod_ref[1:2, :])
    p = _dot(h.astype(BF16), w_ref[...])
    nq = A_Q_HEADS * LANES
    q = p[:, :nq]
    k = p[:, nq:nq + LANES]
    v = p[:, nq + LANES:nq + 2 * LANES]
    z = p[:, nq + 2 * LANES:]
    gmat = gmat_ref[...]
    gmat64 = gmat64_ref[...]
    cos = cos_ref[...]
    sin = sin_ref[...]

    qn = q * lax.rsqrt(_group_mean_sq(q, gmat) + EPS) * gq_ref[...]
    q_ref[...] = (_rope(qn, cos, sin) * (A_HEAD_DIM ** -0.5 * LOG2E)).astype(BF16)
    kn = k * lax.rsqrt(_group_mean_sq(k, gmat64[:LANES, :LANES]) + EPS) * gk_ref[...]
    k_ref[...] = _rope(kn, cos, sin).astype(BF16)
    low = lax.broadcasted_iota(jnp.int32, v.shape, 1) < A_HEAD_DIM
    v_ref[:, :LANES] = jnp.where(low, v, 1.0).astype(BF16)
    v_ref[:, LANES:] = jnp.where(low, pltpu.roll(v, A_HEAD_DIM, 1), 1.0).astype(BF16)

    ge = 0.5 * z * (1.0 + jnp.tanh(np.sqrt(2.0 / np.pi).astype(np.float32)
                                   * (z + 0.044715 * (z * z * z))))
    u = ge[:, :MIX_HALF]
    vv = ge[:, MIX_HALF:]
    vn = vv * lax.rsqrt(_group_mean_sq(vv, gmat64) + EPS) * sgug_ref[...]
    vnb = vn.astype(BF16)
    lane = lax.broadcasted_iota(jnp.int32, (B_CHUNK, LANES), 1)
    low = lane < B_GROUP_DIM
    rows = []
    for c in range(x.shape[0] // B_CHUNK):
        slabs = []
        for s in range(MIX_HALF // LANES):
            vs = vnb[c * B_CHUNK:(c + 1) * B_CHUNK, s * LANES:(s + 1) * LANES]
            r0 = _dot(sguw_ref[2 * s], vs)
            r1 = _dot(sguw_ref[2 * s + 1], vs)
            slabs.append(jnp.where(low, r0, r1))
        rows.append(jnp.concatenate(slabs, axis=-1) + sgub_ref[...])
    sv = jnp.concatenate(rows, axis=0)
    mb_ref[...] = (u * sv).astype(BF16)


def _even_in_call(ctx, x, mod, layer, g1, w_in, gq, gk, cos, sin, gmat, gmat64, sgug, sguw, sgub):
    b, seq, d = x.shape
    ns = IN_SAMPLES
    n_ctx_rows = ctx.shape[1]
    ctx_tiles = n_ctx_rows // TM
    t = n_ctx_rows + seq
    nt = t // TM
    nq = A_Q_HEADS * LANES
    full = lambda a: pl.BlockSpec(a.shape, lambda i, j: (0,) * a.ndim)
    return pl.pallas_call(
        functools.partial(_even_in_kernel, ctx_tiles=ctx_tiles),
        out_shape=(
            jax.ShapeDtypeStruct((b, t, nq), BF16),
            jax.ShapeDtypeStruct((b, t, LANES), BF16),
            jax.ShapeDtypeStruct((b, t, A_KV_HEADS * LANES), BF16),
            jax.ShapeDtypeStruct((b, t, MIX_HALF), BF16),
        ),
        grid=(b // ns, nt),
        in_specs=[
            pl.BlockSpec((ns, TM, d), lambda i, j: (i, jnp.minimum(j, ctx_tiles - 1), 0)),
            pl.BlockSpec((ns, TM, d), lambda i, j: (i, jnp.maximum(j - ctx_tiles, 0), 0)),
            pl.BlockSpec((None, ns, N_MOD, d),
                         lambda i, j: (layer, jnp.where(j < ctx_tiles, b // ns, i), 0, 0)),
            full(g1),
            pl.BlockSpec(memory_space=pltpu.VMEM),
            full(gq), full(gk),
            pl.BlockSpec((TM, LANES), lambda i, j: (j, 0)),
            pl.BlockSpec((TM, LANES), lambda i, j: (j, 0)),
            full(gmat), full(gmat64), full(sgug), full(sguw), full(sgub),
        ],
        out_specs=(
            pl.BlockSpec((ns, TM, nq), _q_tile_index(ctx_tiles, nt)),
            pl.BlockSpec((ns, TM, LANES), lambda i, j: (i, j, 0)),
            pl.BlockSpec((ns, TM, A_KV_HEADS * LANES), lambda i, j: (i, j, 0)),
            pl.BlockSpec((ns, TM, MIX_HALF), lambda i, j: (i, j, 0)),
        ),
        compiler_params=pltpu.CompilerParams(
            dimension_semantics=("arbitrary", "arbitrary"), vmem_limit_bytes=VMEM_LIMIT),
        name="even_in_proj",
    )(ctx, x, mod, g1, w_in, gq, gk, cos, sin, gmat, gmat64, sgug, sguw, sgub)


def _attn_body(q_ref, k_ref, v_ref, o_ref, n_keys, groups):
    for r0 in range(0, q_ref.shape[0], TM):
        _attn_rows(q_ref.at[r0:r0 + TM, :], k_ref, v_ref, o_ref.at[r0:r0 + TM, :], n_keys, groups)


def _attn_rows(q_ref, k_ref, v_ref, o_ref, n_keys, groups):
    tm = q_ref.shape[0]
    half = LANES // 2
    low = lax.broadcasted_iota(jnp.int32, (tm, LANES), 1) < half
    res = {}
    for heads, ks, vs in groups:
        q = [q_ref[:, h * LANES:(h + 1) * LANES] for h in heads]
        q = q[0] if len(q) == 1 else jnp.concatenate(q, axis=0)
        s = _dot_nt(q, k_ref[:n_keys, ks * LANES:(ks + 1) * LANES])
        p = jnp.exp2(s - jnp.max(s, axis=-1, keepdims=True))
        vp = vs // 2 * 2
        pv = _dot(p.astype(BF16), v_ref[:n_keys, vp * LANES:(vp + 2) * LANES])
        pv = pv[:, (vs - vp) * LANES:(vs - vp + 1) * LANES]
        sw = pltpu.roll(pv, half, 1)
        for i, h in enumerate(heads):
            a, b = pv[i * tm:(i + 1) * tm], sw[i * tm:(i + 1) * tm]
            res[h] = a / b if h % 2 == 0 else b / a
    for pair in range(len(res) // 2):
        o_ref[:, pair * LANES:(pair + 1) * LANES] = jnp.where(
            low, res[2 * pair], res[2 * pair + 1]).astype(BF16)


def _conv_prepare(yp, yc, yn, has_prev, has_next, ybuf, ysh):
    tm = yc.shape[0]
    ybuf[0:HALO, :] = jnp.where(has_prev, yp, 0.0)
    ybuf[HALO:HALO + tm, :] = yc
    ybuf[HALO + tm:, :] = jnp.where(has_next, yn, 0.0)
    n_sh = ysh.shape[1]
    for s in range(1, SUBLANES):
        ysh[s - 1] = ybuf[s:s + n_sh, :]


def _conv_ln_silu_rows(r0, ybuf, ysh, cw_ref, cb_ref, lng_ref, lnb_ref):
    base = HALO - D_CONV // 2
    nch = ybuf.shape[1]
    acc = None
    for tap in range(D_CONV):
        a, s = divmod(base + tap, SUBLANES)
        lo = SUBLANES * a + r0
        win = ybuf[lo:lo + CONV_ROWS, :] if s == 0 else ysh[s - 1, lo:lo + CONV_ROWS, :]
        term = win.reshape(CONV_ROWS // SUBLANES, SUBLANES, nch) * cw_ref[tap]
        acc = term if acc is None else acc + term
    y = acc.reshape(CONV_ROWS, nch) + cb_ref[...]
    mu = jnp.mean(y, axis=-1, keepdims=True)
    yc = y - mu
    var = jnp.mean(yc * yc, axis=-1, keepdims=True)
    ln = yc * lax.rsqrt(var + EPS) * lng_ref[...] + lnb_ref[...]
    return (ln * (1.0 / (1.0 + jnp.exp(-ln)))).astype(BF16)


def _attn_kernel(q_ref, k_ref, v_ref, o_ref, *, groups):
    _attn_body(q_ref, k_ref, v_ref, o_ref, k_ref.shape[0], groups)


def _attn_call(q, k, v, seq, groups, name):
    b, t, nq = q.shape
    rows = ATT_TILES * TM
    return pl.pallas_call(
        functools.partial(_attn_kernel, groups=groups),
        out_shape=jax.ShapeDtypeStruct((b, seq, MIX_HALF), BF16),
        grid=(b, seq // rows),
        in_specs=[
            pl.BlockSpec((None, rows, nq), lambda i, j: (i, j, 0)),
            pl.BlockSpec((None, t, k.shape[-1]), lambda i, j: (i, 0, 0)),
            pl.BlockSpec((None, t, v.shape[-1]), lambda i, j: (i, 0, 0)),
        ],
        out_specs=pl.BlockSpec((None, rows, MIX_HALF), lambda i, j: (i, j, 0)),
        compiler_params=pltpu.CompilerParams(
            dimension_semantics=("arbitrary", "arbitrary"), vmem_limit_bytes=VMEM_LIMIT),
        name=name,
    )(q, k, v)


def _ctx_attn_call(q, k, v, seq, n_ctx, groups, name):
    b, t, nq = q.shape
    assert seq % n_ctx == 0
    return pl.pallas_call(
        functools.partial(_attn_kernel, groups=groups),
        out_shape=jax.ShapeDtypeStruct((b, n_ctx, MIX_HALF), BF16),
        grid=(b,),
        in_specs=[
            pl.BlockSpec((None, n_ctx, nq), lambda i: (i, seq // n_ctx, 0)),
            pl.BlockSpec((None, n_ctx, k.shape[-1]), lambda i: (i, 0, 0)),
            pl.BlockSpec((None, n_ctx, v.shape[-1]), lambda i: (i, 0, 0)),
        ],
        out_specs=pl.BlockSpec((None, n_ctx, MIX_HALF), lambda i: (i, 0, 0)),
        compiler_params=pltpu.CompilerParams(
            dimension_semantics=("arbitrary",), vmem_limit_bytes=VMEM_LIMIT),
        name=name,
    )(q, k, v)


def _odd_in_kernel(x_ref, mod_ref, g1_ref, w_ref, gcq_ref, gckv_ref, wuq_ref, wukv_ref,
                   cos_ref, sin_ref, q_ref, k_ref, v_ref, y_ref):
    for s in range(x_ref.shape[0]):
        _odd_in_tile(x_ref.at[s], mod_ref.at[s], g1_ref, w_ref, gcq_ref, gckv_ref, wuq_ref, wukv_ref,
                     cos_ref, sin_ref, q_ref.at[s], k_ref.at[s], v_ref.at[s], y_ref.at[s])


def _odd_in_tile(x_ref, mod_ref, g1_ref, w_ref, gcq_ref, gckv_ref, wuq_ref, wukv_ref,
                 cos_ref, sin_ref, q_ref, k_ref, v_ref, y_ref):
    x = x_ref[...]
    h = _modulate(x, g1_ref[...], mod_ref[0:1, :], mod_ref[1:2, :])
    p = _dot(h.astype(BF16), w_ref[...])
    cq = p[:, :C_Q_RANK]
    ckv = p[:, C_Q_RANK:C_Q_RANK + C_KV_RANK]
    o = C_Q_RANK + C_KV_RANK
    kr = p[:, o:o + LANES]
    z = p[:, o + LANES:]
    cos = cos_ref[...]
    sin = sin_ref[...]

    q = _dot((_rms(cq) * gcq_ref[...]).astype(BF16), wuq_ref[...])
    q_ref[...] = (_rope(q, cos, sin) * ((C_NOPE + C_ROPE) ** -0.5 * LOG2E)).astype(BF16)
    kv = _dot((_rms(ckv) * gckv_ref[...]).astype(BF16), wukv_ref[...])
    nk = C_HEADS * LANES
    krr = _rope(kr, cos, sin)
    k_ref[...] = (kv[:, :nk] + jnp.concatenate([krr] * C_HEADS, axis=-1)).astype(BF16)
    v = kv[:, nk:]
    high = lax.broadcasted_iota(jnp.int32, v.shape, 1) % LANES >= C_V
    v_ref[...] = jnp.where(high, 1.0, v).astype(BF16)
    a = z[:, :MIX_HALF]
    gt = z[:, MIX_HALF:]
    y_ref[...] = a * (1.0 / (1.0 + jnp.exp(-gt)))


def _odd_in_call(xs, mod, layer, g1, w_in, gcq, gckv, wuq, wukv, cos, sin, n_ctx_rows):
    b, t, d = xs.shape
    ns = IN_SAMPLES
    nt = t // TM
    nk = C_HEADS * LANES
    full = lambda a: pl.BlockSpec(a.shape, lambda i, j: (0,) * a.ndim)
    return pl.pallas_call(
        _odd_in_kernel,
        out_shape=(
            jax.ShapeDtypeStruct((b, t, nk), BF16),
            jax.ShapeDtypeStruct((b, t, nk), BF16),
            jax.ShapeDtypeStruct((b, t, nk), BF16),
            jax.ShapeDtypeStruct((b, t, MIX_HALF), F32),
        ),
        grid=(b // ns, nt),
        in_specs=[
            pl.BlockSpec((ns, TM, d), lambda i, j: (i, j, 0)),
            pl.BlockSpec((None, ns, N_MOD, d),
                         lambda i, j: (layer, jnp.where(j < n_ctx_rows // TM, b // ns, i), 0, 0)),
            full(g1),
            pl.BlockSpec(memory_space=pltpu.VMEM),
            full(gcq), full(gckv),
            pl.BlockSpec(memory_space=pltpu.VMEM),
            pl.BlockSpec(memory_space=pltpu.VMEM),
            pl.BlockSpec((TM, LANES), lambda i, j: (j, 0)),
            pl.BlockSpec((TM, LANES), lambda i, j: (j, 0)),
        ],
        out_specs=(
            pl.BlockSpec((ns, TM, nk), _q_tile_index(n_ctx_rows // TM, nt)),
            pl.BlockSpec((ns, TM, nk), lambda i, j: (i, j, 0)),
            pl.BlockSpec((ns, TM, nk), lambda i, j: (i, j, 0)),
            pl.BlockSpec((ns, TM, MIX_HALF), lambda i, j: (i, j, 0)),
        ),
        compiler_params=pltpu.CompilerParams(
            dimension_semantics=("arbitrary", "arbitrary"), vmem_limit_bytes=VMEM_LIMIT),
        name="odd_in_proj",
    )(xs, mod, g1, w_in, gcq, gckv, wuq, wukv, cos, sin)


def _mlp(h2, w1_ref, w2_ref):
    acc = None
    for c in range(0, FF_DIM, FF_CHUNK):
        hc = jnp.maximum(_dot(h2, w1_ref[:, c:c + FF_CHUNK]), 0.0)
        part = _dot((hc * hc).astype(BF16), w2_ref[c:c + FF_CHUNK, :])
        acc = part if acc is None else acc + part
    return acc


def _even_out_kernel(ctx_ref, x_ref, actx_ref, alat_ref, mb_ref, mod_ref, g2_ref, wo_ref, w1_ref,
                     w2_ref, o_ref, *, ctx_tiles):
    ns, tm = x_ref.shape[0], x_ref.shape[1]
    is_ctx = pl.program_id(1) < ctx_tiles
    stack = lambda parts: jnp.concatenate(parts, axis=0)
    attn = jnp.where(is_ctx, stack([actx_ref[s] for s in range(ns)]),
                     stack([alat_ref[s] for s in range(ns)]))
    a = (_dot(attn, wo_ref[:MIX_HALF, :])
         + _dot(stack([mb_ref[s] for s in range(ns)]), wo_ref[MIX_HALF:, :]))
    x1 = [jnp.where(is_ctx, ctx_ref[s], x_ref[s]) + mod_ref[s, 2:3, :] * a[s * tm:(s + 1) * tm]
          for s in range(ns)]
    h2 = stack([_modulate(x1[s], g2_ref[...], mod_ref[s, 3:4, :], mod_ref[s, 4:5, :])
                for s in range(ns)]).astype(BF16)
    acc = _mlp(h2, w1_ref, w2_ref)
    for s in range(ns):
        o_ref[s] = x1[s] + mod_ref[s, 5:6, :] * acc[s * tm:(s + 1) * tm]


def _odd_out_kernel(x_ref, attn_ref, yp_ref, yc_ref, yn_ref, mod_ref, g2_ref, wo_ref, w1_ref, w2_ref,
                    cw_ref, cb_ref, lng_ref, lnb_ref, fg_ref, o_ref, ybuf, ysh, *,
                    tile0, ctx_tiles, last_tile, final_norm):
    j = pl.program_id(1) + tile0
    has_prev = jnp.logical_and(j > 0, j != ctx_tiles)
    has_next = jnp.logical_and(j < last_tile, j != ctx_tiles - 1)
    ns, tm = x_ref.shape[0], x_ref.shape[1]
    for s in range(ns):
        _conv_prepare(yp_ref[s], yc_ref[s], yn_ref[s], has_prev, has_next, ybuf.at[s], ysh.at[s])
        mixb = jnp.concatenate(
            [_conv_ln_silu_rows(r0, ybuf.at[s], ysh.at[s], cw_ref, cb_ref, lng_ref, lnb_ref)
             for r0 in range(0, tm, CONV_ROWS)], axis=0)
        a = _dot(attn_ref[s], wo_ref[:MIX_HALF, :]) + _dot(mixb, wo_ref[MIX_HALF:, :])
        x1 = x_ref[s] + mod_ref[s, 2:3, :] * a
        h2 = _modulate(x1, g2_ref[...], mod_ref[s, 3:4, :], mod_ref[s, 4:5, :]).astype(BF16)
        out = x1 + mod_ref[s, 5:6, :] * _mlp(h2, w1_ref, w2_ref)
        if final_norm:
            out = _rms(out) * fg_ref[...]
        o_ref[s] = out


def _out_specs(b, d, layer, ctx_tiles, tile0):
    ns = OUT_SAMPLES
    half_spec = pl.BlockSpec((ns, TM, MIX_HALF), lambda i, j: (i, j, 0))
    mod_spec = pl.BlockSpec((None, ns, N_MOD, d),
                            lambda i, j: (layer, jnp.where(j + tile0 < ctx_tiles, b // ns, i), 0, 0))
    out_spec = pl.BlockSpec((ns, TM, d), lambda i, j: (i, j, 0))
    params = pltpu.CompilerParams(
        dimension_semantics=("arbitrary", "arbitrary"), vmem_limit_bytes=VMEM_LIMIT)
    return half_spec, mod_spec, out_spec, params


def _even_out_call(ctx, x, attn_ctx, attn_lat, mixb, mod, layer, g2, wo, w1, w2):
    b, seq, d = x.shape
    ns = OUT_SAMPLES
    ctx_tiles = ctx.shape[1] // TM
    nt = ctx_tiles + seq // TM
    full = lambda a: pl.BlockSpec(a.shape, lambda i, j: (0,) * a.ndim)
    whole = pl.BlockSpec(memory_space=pltpu.VMEM)
    half_spec, mod_spec, out_spec, params = _out_specs(b, d, layer, ctx_tiles, 0)
    return pl.pallas_call(
        functools.partial(_even_out_kernel, ctx_tiles=ctx_tiles),
        out_shape=jax.ShapeDtypeStruct((b, nt * TM, d), F32),
        grid=(b // ns, nt),
        in_specs=[pl.BlockSpec((ns, TM, d), lambda i, j: (i, jnp.minimum(j, ctx_tiles - 1), 0)),
                  pl.BlockSpec((ns, TM, d), lambda i, j: (i, jnp.maximum(j - ctx_tiles, 0), 0)),
                  pl.BlockSpec((ns, TM, MIX_HALF),
                               lambda i, j: (i, jnp.minimum(j, ctx_tiles - 1), 0)),
                  pl.BlockSpec((ns, TM, MIX_HALF),
                               lambda i, j: (i, jnp.maximum(j - ctx_tiles, 0), 0)),
                  half_spec, mod_spec, full(g2), whole, whole, whole],
        out_specs=out_spec, compiler_params=params, name="even_out_mlp",
    )(ctx, x, attn_ctx, attn_lat, mixb, mod, g2, wo, w1, w2)


def _odd_out_call(xs, attn, y, mod, layer, g2, wo, w1, w2, cw, cb, lng, lnb, fg, n_ctx_rows, tile0,
                  final_norm):
    b, t, d = xs.shape
    ns = OUT_SAMPLES
    nt = t // TM - tile0
    ctx_tiles = n_ctx_rows // TM
    full = lambda a: pl.BlockSpec(a.shape, lambda i, j: (0,) * a.ndim)
    whole = pl.BlockSpec(memory_space=pltpu.VMEM)
    half_spec, mod_spec, out_spec, params = _out_specs(b, d, layer, ctx_tiles, tile0)
    hb = TM // HALO
    last_halo = t // HALO - 1
    kern = functools.partial(_odd_out_kernel, tile0=tile0, ctx_tiles=ctx_tiles,
                             last_tile=t // TM - 1, final_norm=final_norm)
    return pl.pallas_call(
        kern,
        out_shape=jax.ShapeDtypeStruct((b, nt * TM, d), F32),
        grid=(b // ns, nt),
        in_specs=[pl.BlockSpec((ns, TM, d), lambda i, j: (i, j + tile0, 0)),
                  half_spec,
                  pl.BlockSpec((ns, HALO, MIX_HALF),
                               lambda i, j: (i, jnp.maximum((j + tile0) * hb - 1, 0), 0)),
                  pl.BlockSpec((ns, TM, MIX_HALF), lambda i, j: (i, j + tile0, 0)),
                  pl.BlockSpec((ns, HALO, MIX_HALF),
                               lambda i, j: (i, jnp.minimum((j + tile0 + 1) * hb, last_halo), 0)),
                  mod_spec, full(g2), whole, whole, whole,
                  full(cw), full(cb), full(lng), full(lnb), full(fg)],
        out_specs=out_spec,
        scratch_shapes=[pltpu.VMEM((ns, TM + 2 * HALO, MIX_HALF), F32),
                        pltpu.VMEM((ns, SUBLANES - 1, TM + 2 * HALO - SUBLANES, MIX_HALF), F32)],
        compiler_params=params, name="odd_out_mlp",
    )(xs, attn, y, y, y, mod, g2, wo, w1, w2, cw, cb, lng, lnb, fg)


def _axial_angles(length, d_rot):
    rows = length // GRID_W
    row = np.repeat(np.arange(rows), GRID_W).astype(np.float64)
    col = np.tile(np.arange(GRID_W), rows).astype(np.float64)
    d_axis = d_rot // 2
    inv = ROPE_THETA ** (-np.arange(0, d_axis, 2, dtype=np.float64) / d_axis)
    return np.concatenate([row[:, None] * inv, col[:, None] * inv], axis=-1)


def _rope_tables(seq, n_ctx, d_rot, lane0, period):
    ang = _axial_angles(seq, d_rot)
    cos = np.repeat(np.cos(ang), 2, axis=-1)
    sin = np.repeat(np.sin(ang), 2, axis=-1) * np.tile(np.array([-1.0, 1.0]), d_rot // 2)
    cos_p = np.ones((seq, period))
    sin_p = np.zeros((seq, period))
    cos_p[:, lane0:lane0 + d_rot] = cos
    sin_p[:, lane0:lane0 + d_rot] = sin
    cos_p = np.tile(cos_p, (1, LANES // period))
    sin_p = np.tile(sin_p, (1, LANES // period))
    cos_t = np.concatenate([np.ones((n_ctx, LANES)), cos_p], axis=0)
    sin_t = np.concatenate([np.zeros((n_ctx, LANES)), sin_p], axis=0)
    return jnp.asarray(cos_t, F32), jnp.asarray(sin_t, F32)


def _even_weights(w_in, q_g, k_g, sgu_g, sgu_w, sgu_b):
    d = w_in.shape[0]
    ev_q = A_Q_HEADS * A_HEAD_DIM
    ev_kv = A_KV_HEADS * A_HEAD_DIM
    wq = w_in[:, :ev_q].reshape(d, A_Q_HEADS, A_HEAD_DIM)
    slabs = []
    gq = []
    zero = jnp.zeros((d, A_HEAD_DIM), w_in.dtype)
    zg = jnp.zeros((A_HEAD_DIM,), q_g.dtype)
    for h in range(A_Q_HEADS):
        if h // A_GROUP == 0:
            slabs += [wq[:, h], zero]
            gq += [q_g, zg]
        else:
            slabs += [zero, wq[:, h]]
            gq += [zg, q_g]
    w = jnp.concatenate(slabs + [w_in[:, ev_q:]], axis=1).astype(BF16)
    gq = jnp.concatenate(gq)[None, :]
    gk = jnp.tile(k_g, A_KV_HEADS)[None, :]
    sgug = sgu_g.reshape(1, -1)
    sgub = jnp.repeat(sgu_b.T, B_GROUP_DIM, axis=1)
    return w, gq, gk, sgug, sgu_w.astype(BF16), sgub


def _odd_weights(w_in, w_uq, w_ukv):
    d = w_in.shape[0]
    c0 = C_Q_RANK + C_KV_RANK
    w_kr = w_in[:, c0:c0 + C_ROPE]
    kr_slab = jnp.zeros((d, LANES), w_in.dtype).at[:, C_NOPE:C_NOPE + C_ROPE].set(w_kr)
    w = jnp.concatenate([w_in[:, :c0], kr_slab, w_in[:, c0 + C_ROPE:]], axis=1).astype(BF16)
    uq = w_uq.reshape(C_Q_RANK, C_HEADS, C_NOPE + C_ROPE)
    uq = jnp.pad(uq, ((0, 0), (0, 0), (0, LANES - C_NOPE - C_ROPE))).reshape(C_Q_RANK, C_HEADS * LANES)
    ukv = w_ukv.reshape(C_KV_RANK, C_HEADS, C_NOPE + C_V)
    uk = jnp.pad(ukv[:, :, :C_NOPE], ((0, 0), (0, 0), (0, LANES - C_NOPE))).reshape(C_KV_RANK, -1)
    uv = jnp.pad(ukv[:, :, C_NOPE:], ((0, 0), (0, 0), (0, LANES - C_V))).reshape(C_KV_RANK, -1)
    return w, uq.astype(BF16), jnp.concatenate([uk, uv], axis=1).astype(BF16)


def _group_matrix(group_lanes):
    idx = np.arange(2 * LANES) // group_lanes
    return jnp.asarray((idx[:, None] == idx[None, :]).astype(np.float32) / A_HEAD_DIM, BF16)


def kernel(x, c, ctx, c_ctx, ada_w, ada_b, norm1_g, norm2_g, w_out, mlp_w1, mlp_w2, ev_w_in, ev_q_norm_g, ev_k_norm_g, ev_sgu_norm_g, ev_sgu_w, ev_sgu_b, od_w_in, od_q_norm_g, od_kv_norm_g, od_w_uq, od_w_ukv, od_conv_w, od_conv_b, od_ln_g, od_ln_b, final_g):
    b, seq, d = x.shape
    n_ctx = ctx.shape[1]
    depth = ada_w.shape[0]
    assert d == D_MODEL and seq % TM == 0 and n_ctx % TM == 0 and depth == 2
    ctx_tiles = n_ctx // TM

    n_rep = max(IN_SAMPLES, OUT_SAMPLES)
    assert b % IN_SAMPLES == 0 and b % OUT_SAMPLES == 0
    r = -(-(b + n_rep) // SUBLANES) * SUBLANES
    cvec = jnp.concatenate([c] + [c_ctx[None, :]] * n_rep
                           + [jnp.zeros((r - b - n_rep, d), F32)], axis=0)
    mod = _ada_call(cvec, ada_w, ada_b).reshape(depth, r, N_MOD, d)
    fg = final_g[None, :]

    wo = w_out.astype(BF16)
    w1 = mlp_w1.astype(BF16)
    w2 = mlp_w2.astype(BF16)

    cos_e, sin_e = _rope_tables(seq, n_ctx, A_HEAD_DIM, 0, A_HEAD_DIM)
    w_e, gq, gk, sgug, sguw, sgub = _even_weights(ev_w_in[0], ev_q_norm_g[0], ev_k_norm_g[0],
                                                  ev_sgu_norm_g[0], ev_sgu_w[0], ev_sgu_b[0])
    q, k, v, mixb = _even_in_call(ctx, x, mod, 0, norm1_g[0][None, :], w_e, gq, gk, cos_e, sin_e,
                                  _group_matrix(LANES), _group_matrix(A_HEAD_DIM), sgug, sguw, sgub)
    groups = tuple(((h,), 0, h // A_GROUP) for h in range(A_Q_HEADS))
    attn_lat = _attn_call(q, k, v, seq, groups, "even_attention")
    attn_ctx = _ctx_attn_call(q, k, v, seq, n_ctx, groups, "even_ctx_attention")
    xs = _even_out_call(ctx, x, attn_ctx, attn_lat, mixb, mod, 0, norm2_g[0][None, :], wo[0], w1[0],
                        w2[0])

    cos_o, sin_o = _rope_tables(seq, n_ctx, C_ROPE, C_NOPE, LANES)
    w_o, uq, ukv = _odd_weights(od_w_in[0], od_w_uq[0], od_w_ukv[0])
    q, k, v, y = _odd_in_call(xs, mod, 1, norm1_g[1][None, :], w_o, od_q_norm_g[0][None, :],
                              od_kv_norm_g[0][None, :], uq, ukv, cos_o, sin_o, n_ctx)
    groups = tuple(((h,), h, h) for h in range(C_HEADS))
    cw = jnp.broadcast_to(od_conv_w[0][:, None, :], (D_CONV, SUBLANES, MIX_HALF))
    attn = _attn_call(q, k, v, seq, groups, "odd_attention")
    return _odd_out_call(xs, attn, y, mod, 1, norm2_g[1][None, :], wo[1], w1[1], w2[1], cw,
                         od_conv_b[0][None, :], od_ln_g[0][None, :], od_ln_b[0][None, :], fg, n_ctx,
                         ctx_tiles, True)
```

```python
import functools

import numpy as np
import jax
import jax.numpy as jnp
from jax import lax
from jax.experimental import pallas as pl
from jax.experimental.pallas import tpu as pltpu

F32 = jnp.float32
BF16 = jnp.bfloat16

D_MODEL = 1024
GRID_W = 64
ROPE_THETA = 10000.0
EPS = 1e-6
MIX_HALF = D_MODEL // 2
N_MOD = 6

A_HEAD_DIM = 64
A_Q_HEADS = 8
A_KV_HEADS = 2
A_GROUP = A_Q_HEADS // A_KV_HEADS
B_GROUPS = 8
B_GROUP_DIM = 64
B_CHUNK = 128
C_HEADS = 8
C_NOPE = 64
C_ROPE = 32
C_V = 64
C_Q_RANK = 256
C_KV_RANK = 128
D_CONV = 31
FF_DIM = 4 * D_MODEL

LANES = 128
TM = 256
ADA_BLOCK_N = 1536
FF_CHUNK = 1024
SUBLANES = 8
ATT_TILES = 2
IN_SAMPLES = 4
OUT_SAMPLES = 2
CONV_ROWS = 32
HALO = 16
VMEM_LIMIT = 56 * 1024 * 1024
LOG2E = 1.4426950408889634


def _dot(a, b):
    return jnp.dot(a, b, preferred_element_type=F32)


def _dot_nt(a, b):
    return lax.dot_general(a, b, (((1,), (1,)), ((), ())), preferred_element_type=F32)


def _rms(x):
    return x * lax.rsqrt(jnp.mean(x * x, axis=-1, keepdims=True) + EPS)


def _modulate(x, g, shift, scale):
    return (_rms(x) * g) * (1.0 + scale) + shift


def _pair_swap(x):
    even = lax.broadcasted_iota(jnp.int32, (x.shape[0], LANES), 1) % 2 == 0
    slabs = []
    for s in range(0, x.shape[-1], LANES):
        xs = x[:, s:s + LANES]
        slabs.append(jnp.where(even, pltpu.roll(xs, LANES - 1, 1), pltpu.roll(xs, 1, 1)))
    return slabs[0] if len(slabs) == 1 else jnp.concatenate(slabs, axis=-1)


def _rope(x, cos, sin_signed):
    reps = x.shape[-1] // LANES
    if reps > 1:
        cos = jnp.concatenate([cos] * reps, axis=-1)
        sin_signed = jnp.concatenate([sin_signed] * reps, axis=-1)
    return x * cos + _pair_swap(x) * sin_signed


def _q_tile_index(ctx_tiles, nt):
    return lambda i, j: (i, jnp.where(j < ctx_tiles, j + nt - ctx_tiles, j - ctx_tiles), 0)


def _group_mean_sq(x, gmat):
    w = gmat.shape[0]
    x2 = (x * x).astype(BF16)
    parts = [_dot(x2[:, i:i + w], gmat) for i in range(0, x.shape[-1], w)]
    return parts[0] if len(parts) == 1 else jnp.concatenate(parts, axis=-1)


def _ada_kernel(c_ref, w_ref, b_ref, o_ref):
    c = c_ref[...]
    s = c * (1.0 / (1.0 + jnp.exp(-c)))
    o_ref[...] = _dot(s.astype(BF16), w_ref[...].astype(BF16)) + b_ref[...]


def _ada_call(cvec, ada_w, ada_b):
    depth, d, n = ada_w.shape
    r = cvec.shape[0]
    bn = ADA_BLOCK_N
    assert n % bn == 0
    return pl.pallas_call(
        _ada_kernel,
        out_shape=jax.ShapeDtypeStruct((depth, r, n), F32),
        grid=(depth, n // bn),
        in_specs=[
            pl.BlockSpec((r, d), lambda i, j: (0, 0)),
            pl.BlockSpec((None, d, bn), lambda i, j: (i, 0, j)),
            pl.BlockSpec((None, 1, bn), lambda i, j: (i, 0, j)),
        ],
        out_specs=pl.BlockSpec((None, r, bn), lambda i, j: (i, 0, j)),
        compiler_params=pltpu.CompilerParams(
            dimension_semantics=("arbitrary", "arbitrary"), vmem_limit_bytes=VMEM_LIMIT),
        name="ada_mod",
    )(cvec, ada_w, ada_b.reshape(depth, 1, n))


def _even_in_kernel(ctx_ref, x_ref, mod_ref, g1_ref, w_ref, gq_ref, gk_ref, cos_ref, sin_ref,
                    gmat64_ref, sgug_ref, sguw_ref, sgub_ref, q_ref, k_ref, v_ref, mb_ref,
                    *, ctx_tiles):
    for s in range(x_ref.shape[0]):
        _even_in_tile(ctx_ref.at[s], x_ref.at[s], mod_ref.at[s], g1_ref, w_ref, gq_ref, gk_ref,
                      cos_ref, sin_ref, gmat64_ref, sgug_ref, sguw_ref, sgub_ref,
                      q_ref.at[s], k_ref.at[s], v_ref.at[s], mb_ref.at[s], ctx_tiles)


def _even_in_tile(ctx_ref, x_ref, mod_ref, g1_ref, w_ref, gq_ref, gk_ref, cos_ref, sin_ref,
                  gmat64_ref, sgug_ref, sguw_ref, sgub_ref, q_ref, k_ref, v_ref, mb_ref,
                  ctx_tiles):
    x = jnp.where(pl.program_id(1) < ctx_tiles, ctx_ref[...], x_ref[...])
    h = _modulate(x, g1_ref[...], mod_ref[0:1, :], mod_ref[1:2, :])
    p = _dot(h.astype(BF16), w_ref[...])
    nq = A_Q_HEADS * A_HEAD_DIM
    q = p[:, :nq]
    k = p[:, nq:nq + LANES]
    v = p[:, nq + LANES:nq + 2 * LANES]
    z = p[:, nq + 2 * LANES:]
    gmat64 = gmat64_ref[...]
    cos = cos_ref[...]
    sin = sin_ref[...]

    qn = q * lax.rsqrt(_group_mean_sq(q, gmat64) + EPS) * gq_ref[...]
    q_ref[...] = (_rope(qn, cos, sin) * (A_HEAD_DIM ** -0.5 * LOG2E)).astype(BF16)
    kn = k * lax.rsqrt(_group_mean_sq(k, gmat64[:LANES, :LANES]) + EPS) * gk_ref[...]
    k_ref[...] = _rope(kn, cos, sin).astype(BF16)
    low = lax.broadcasted_iota(jnp.int32, v.shape, 1) < A_HEAD_DIM
    v_ref[:, :LANES] = jnp.where(low, v, 1.0).astype(BF16)
    v_ref[:, LANES:] = jnp.where(low, pltpu.roll(v, A_HEAD_DIM, 1), 1.0).astype(BF16)

    ge = 0.5 * z * (1.0 + jnp.tanh(np.sqrt(2.0 / np.pi).astype(np.float32)
                                   * (z + 0.044715 * (z * z * z))))
    u = ge[:, :MIX_HALF]
    vv = ge[:, MIX_HALF:]
    vn = vv * lax.rsqrt(_group_mean_sq(vv, gmat64) + EPS) * sgug_ref[...]
    vnb = vn.astype(BF16)
    lane = lax.broadcasted_iota(jnp.int32, (B_CHUNK, LANES), 1)
    low = lane < B_GROUP_DIM
    rows = []
    for c in range(x.shape[0] // B_CHUNK):
        slabs = []
        for s in range(MIX_HALF // LANES):
            vs = vnb[c * B_CHUNK:(c + 1) * B_CHUNK, s * LANES:(s + 1) * LANES]
            r0 = _dot(sguw_ref[2 * s], vs)
            r1 = _dot(sguw_ref[2 * s + 1], vs)
            slabs.append(jnp.where(low, r0, r1))
        rows.append(jnp.concatenate(slabs, axis=-1) + sgub_ref[...])
    sv = jnp.concatenate(rows, axis=0)
    mb_ref[...] = (u * sv).astype(BF16)


def _even_in_call(ctx, x, mod, layer, g1, w_in, gq, gk, cos, sin, gmat64, sgug, sguw, sgub):
    b, seq, d = x.shape
    ns = IN_SAMPLES
    n_ctx_rows = ctx.shape[1]
    ctx_tiles = n_ctx_rows // TM
    t = n_ctx_rows + seq
    nt = t // TM
    nq = A_Q_HEADS * A_HEAD_DIM
    full = lambda a: pl.BlockSpec(a.shape, lambda i, j: (0,) * a.ndim)
    return pl.pallas_call(
        functools.partial(_even_in_kernel, ctx_tiles=ctx_tiles),
        out_shape=(
            jax.ShapeDtypeStruct((b, t, nq), BF16),
            jax.ShapeDtypeStruct((b, t, LANES), BF16),
            jax.ShapeDtypeStruct((b, t, A_KV_HEADS * LANES), BF16),
            jax.ShapeDtypeStruct((b, t, MIX_HALF), BF16),
        ),
        grid=(b // ns, nt),
        in_specs=[
            pl.BlockSpec((ns, TM, d), lambda i, j: (i, jnp.minimum(j, ctx_tiles - 1), 0)),
            pl.BlockSpec((ns, TM, d), lambda i, j: (i, jnp.maximum(j - ctx_tiles, 0), 0)),
            pl.BlockSpec((None, ns, N_MOD, d),
                         lambda i, j: (layer, jnp.where(j < ctx_tiles, b // ns, i), 0, 0)),
            full(g1),
            pl.BlockSpec(memory_space=pltpu.VMEM),
            full(gq), full(gk),
            pl.BlockSpec((TM, LANES), lambda i, j: (j, 0)),
            pl.BlockSpec((TM, LANES), lambda i, j: (j, 0)),
            full(gmat64), full(sgug), full(sguw), full(sgub),
        ],
        out_specs=(
            pl.BlockSpec((ns, TM, nq), _q_tile_index(ctx_tiles, nt)),
            pl.BlockSpec((ns, TM, LANES), lambda i, j: (i, j, 0)),
            pl.BlockSpec((ns, TM, A_KV_HEADS * LANES), lambda i, j: (i, j, 0)),
            pl.BlockSpec((ns, TM, MIX_HALF), lambda i, j: (i, j, 0)),
        ),
        compiler_params=pltpu.CompilerParams(
            dimension_semantics=("arbitrary", "arbitrary"), vmem_limit_bytes=VMEM_LIMIT),
        name="even_in_proj",
    )(ctx, x, mod, g1, w_in, gq, gk, cos, sin, gmat64, sgug, sguw, sgub)


def _attn_body(q_ref, k_ref, v_ref, o_ref, n_keys, groups):
    for r0 in range(0, q_ref.shape[0], TM):
        _attn_rows(q_ref.at[r0:r0 + TM, :], k_ref, v_ref, o_ref.at[r0:r0 + TM, :], n_keys, groups)


def _attn_rows(q_ref, k_ref, v_ref, o_ref, n_keys, groups):
    tm = q_ref.shape[0]
    half = LANES // 2
    low = lax.broadcasted_iota(jnp.int32, (tm, LANES), 1) < half
    res = []
    for h, (qs, q_half, ks, vs) in enumerate(groups):
        q = q_ref[:, qs * LANES:(qs + 1) * LANES]
        if q_half is not None:
            q = jnp.where(low if q_half == 0 else jnp.logical_not(low), q.astype(F32), 0.0).astype(BF16)
        s = _dot_nt(q, k_ref[:n_keys, ks * LANES:(ks + 1) * LANES])
        p = jnp.exp2(s - jnp.max(s, axis=-1, keepdims=True))
        vp = vs // 2 * 2
        pv = _dot(p.astype(BF16), v_ref[:n_keys, vp * LANES:(vp + 2) * LANES])
        pv = pv[:, (vs - vp) * LANES:(vs - vp + 1) * LANES]
        sw = pltpu.roll(pv, half, 1)
        res.append(pv / sw if h % 2 == 0 else sw / pv)
    for pair in range(len(res) // 2):
        o_ref[:, pair * LANES:(pair + 1) * LANES] = jnp.where(
            low, res[2 * pair], res[2 * pair + 1]).astype(BF16)


def _conv_prepare(yp, yc, yn, has_prev, has_next, ybuf, ysh):
    tm = yc.shape[0]
    ybuf[0:HALO, :] = jnp.where(has_prev, yp, 0.0)
    ybuf[HALO:HALO + tm, :] = yc
    ybuf[HALO + tm:, :] = jnp.where(has_next, yn, 0.0)
    n_sh = ysh.shape[1]
    for s in range(1, SUBLANES):
        ysh[s - 1] = ybuf[s:s + n_sh, :]


def _conv_ln_silu_rows(r0, ybuf, ysh, cw_ref, cb_ref, lng_ref, lnb_ref):
    base = HALO - D_CONV // 2
    nch = ybuf.shape[1]
    acc = None
    for tap in range(D_CONV):
        a, s = divmod(base + tap, SUBLANES)
        lo = SUBLANES * a + r0
        win = ybuf[lo:lo + CONV_ROWS, :] if s == 0 else ysh[s - 1, lo:lo + CONV_ROWS, :]
        term = win.reshape(CONV_ROWS // SUBLANES, SUBLANES, nch) * cw_ref[tap]
        acc = term if acc is None else acc + term
    y = acc.reshape(CONV_ROWS, nch) + cb_ref[...]
    mu = jnp.mean(y, axis=-1, keepdims=True)
    yc = y - mu
    var = jnp.mean(yc * yc, axis=-1, keepdims=True)
    ln = yc * lax.rsqrt(var + EPS) * lng_ref[...] + lnb_ref[...]
    return (ln * (1.0 / (1.0 + jnp.exp(-ln)))).astype(BF16)


def _attn_kernel(q_ref, k_ref, v_ref, o_ref, *, groups):
    _attn_body(q_ref, k_ref, v_ref, o_ref, k_ref.shape[0], groups)


def _attn_call(q, k, v, seq, groups, name):
    b, t, nq = q.shape
    rows = ATT_TILES * TM
    return pl.pallas_call(
        functools.partial(_attn_kernel, groups=groups),
        out_shape=jax.ShapeDtypeStruct((b, seq, MIX_HALF), BF16),
        grid=(b, seq // rows),
        in_specs=[
            pl.BlockSpec((None, rows, nq), lambda i, j: (i, j, 0)),
            pl.BlockSpec((None, t, k.shape[-1]), lambda i, j: (i, 0, 0)),
            pl.BlockSpec((None, t, v.shape[-1]), lambda i, j: (i, 0, 0)),
        ],
        out_specs=pl.BlockSpec((None, rows, MIX_HALF), lambda i, j: (i, j, 0)),
        compiler_params=pltpu.CompilerParams(
            dimension_semantics=("arbitrary", "arbitrary"), vmem_limit_bytes=VMEM_LIMIT),
        name=name,
    )(q, k, v)


def _ctx_attn_call(q, k, v, seq, n_ctx, groups, name):
    b, t, nq = q.shape
    assert seq % n_ctx == 0
    return pl.pallas_call(
        functools.partial(_attn_kernel, groups=groups),
        out_shape=jax.ShapeDtypeStruct((b, n_ctx, MIX_HALF), BF16),
        grid=(b,),
        in_specs=[
            pl.BlockSpec((None, n_ctx, nq), lambda i: (i, seq // n_ctx, 0)),
            pl.BlockSpec((None, n_ctx, k.shape[-1]), lambda i: (i, 0, 0)),
            pl.BlockSpec((None, n_ctx, v.shape[-1]), lambda i: (i, 0, 0)),
        ],
        out_specs=pl.BlockSpec((None, n_ctx, MIX_HALF), lambda i: (i, 0, 0)),
        compiler_params=pltpu.CompilerParams(
            dimension_semantics=("arbitrary",), vmem_limit_bytes=VMEM_LIMIT),
        name=name,
    )(q, k, v)


def _odd_in_kernel(x_ref, mod_ref, g1_ref, w_ref, gcq_ref, gckv_ref, wuq_ref, wukv_ref,
                   cos_ref, sin_ref, q_ref, k_ref, v_ref, y_ref):
    for s in range(x_ref.shape[0]):
        _odd_in_tile(x_ref.at[s], mod_ref.at[s], g1_ref, w_ref, gcq_ref, gckv_ref, wuq_ref, wukv_ref,
                     cos_ref, sin_ref, q_ref.at[s], k_ref.at[s], v_ref.at[s], y_ref.at[s])


def _odd_in_tile(x_ref, mod_ref, g1_ref, w_ref, gcq_ref, gckv_ref, wuq_ref, wukv_ref,
                 cos_ref, sin_ref, q_ref, k_ref, v_ref, y_ref):
    x = x_ref[...]
    h = _modulate(x, g1_ref[...], mod_ref[0:1, :], mod_ref[1:2, :])
    p = _dot(h.astype(BF16), w_ref[...])
    cq = p[:, :C_Q_RANK]
    ckv = p[:, C_Q_RANK:C_Q_RANK + C_KV_RANK]
    o = C_Q_RANK + C_KV_RANK
    kr = p[:, o:o + LANES]
    z = p[:, o + LANES:]
    cos = cos_ref[...]
    sin = sin_ref[...]

    q = _dot((_rms(cq) * gcq_ref[...]).astype(BF16), wuq_ref[...])
    q_ref[...] = (_rope(q, cos, sin) * ((C_NOPE + C_ROPE) ** -0.5 * LOG2E)).astype(BF16)
    kv = _dot((_rms(ckv) * gckv_ref[...]).astype(BF16), wukv_ref[...])
    nk = C_HEADS * LANES
    krr = _rope(kr, cos, sin)
    k_ref[...] = (kv[:, :nk] + jnp.concatenate([krr] * C_HEADS, axis=-1)).astype(BF16)
    v = kv[:, nk:]
    high = lax.broadcasted_iota(jnp.int32, v.shape, 1) % LANES >= C_V
    v_ref[...] = jnp.where(high, 1.0, v).astype(BF16)
    a = z[:, :MIX_HALF]
    gt = z[:, MIX_HALF:]
    y_ref[...] = a * (1.0 / (1.0 + jnp.exp(-gt)))


def _odd_in_call(xs, mod, layer, g1, w_in, gcq, gckv, wuq, wukv, cos, sin, n_ctx_rows):
    b, t, d = xs.shape
    ns = IN_SAMPLES
    nt = t // TM
    nk = C_HEADS * LANES
    full = lambda a: pl.BlockSpec(a.shape, lambda i, j: (0,) * a.ndim)
    return pl.pallas_call(
        _odd_in_kernel,
        out_shape=(
            jax.ShapeDtypeStruct((b, t, nk), BF16),
            jax.ShapeDtypeStruct((b, t, nk), BF16),
            jax.ShapeDtypeStruct((b, t, nk), BF16),
            jax.ShapeDtypeStruct((b, t, MIX_HALF), F32),
        ),
        grid=(b // ns, nt),
        in_specs=[
            pl.BlockSpec((ns, TM, d), lambda i, j: (i, j, 0)),
            pl.BlockSpec((None, ns, N_MOD, d),
                         lambda i, j: (layer, jnp.where(j < n_ctx_rows // TM, b // ns, i), 0, 0)),
            full(g1),
            pl.BlockSpec(memory_space=pltpu.VMEM),
            full(gcq), full(gckv),
            pl.BlockSpec(memory_space=pltpu.VMEM),
            pl.BlockSpec(memory_space=pltpu.VMEM),
            pl.BlockSpec((TM, LANES), lambda i, j: (j, 0)),
            pl.BlockSpec((TM, LANES), lambda i, j: (j, 0)),
        ],
        out_specs=(
            pl.BlockSpec((ns, TM, nk), _q_tile_index(n_ctx_rows // TM, nt)),
            pl.BlockSpec((ns, TM, nk), lambda i, j: (i, j, 0)),
            pl.BlockSpec((ns, TM, nk), lambda i, j: (i, j, 0)),
            pl.BlockSpec((ns, TM, MIX_HALF), lambda i, j: (i, j, 0)),
        ),
        compiler_params=pltpu.CompilerParams(
            dimension_semantics=("arbitrary", "arbitrary"), vmem_limit_bytes=VMEM_LIMIT),
        name="odd_in_proj",
    )(xs, mod, g1, w_in, gcq, gckv, wuq, wukv, cos, sin)


def _mlp(h2, w1_ref, w2_ref):
    acc = None
    for c in range(0, FF_DIM, FF_CHUNK):
        hc = jnp.maximum(_dot(h2, w1_ref[:, c:c + FF_CHUNK]), 0.0)
        part = _dot((hc * hc).astype(BF16), w2_ref[c:c + FF_CHUNK, :])
        acc = part if acc is None else acc + part
    return acc


def _even_out_kernel(ctx_ref, x_ref, actx_ref, alat_ref, mb_ref, mod_ref, g2_ref, wo_ref, w1_ref,
                     w2_ref, o_ref, *, ctx_tiles):
    ns, tm = x_ref.shape[0], x_ref.shape[1]
    is_ctx = pl.program_id(1) < ctx_tiles
    stack = lambda parts: jnp.concatenate(parts, axis=0)
    attn = jnp.where(is_ctx, stack([actx_ref[s] for s in range(ns)]),
                     stack([alat_ref[s] for s in range(ns)]))
    a = (_dot(attn, wo_ref[:MIX_HALF, :])
         + _dot(stack([mb_ref[s] for s in range(ns)]), wo_ref[MIX_HALF:, :]))
    x1 = [jnp.where(is_ctx, ctx_ref[s], x_ref[s]) + mod_ref[s, 2:3, :] * a[s * tm:(s + 1) * tm]
          for s in range(ns)]
    h2 = stack([_modulate(x1[s], g2_ref[...], mod_ref[s, 3:4, :], mod_ref[s, 4:5, :])
                for s in range(ns)]).astype(BF16)
    acc = _mlp(h2, w1_ref, w2_ref)
    for s in range(ns):
        o_ref[s] = x1[s] + mod_ref[s, 5:6, :] * acc[s * tm:(s + 1) * tm]


def _odd_out_kernel(x_ref, attn_ref, yp_ref, yc_ref, yn_ref, mod_ref, g2_ref, wo_ref, w1_ref, w2_ref,
                    cw_ref, cb_ref, lng_ref, lnb_ref, fg_ref, o_ref, ybuf, ysh, *,
                    tile0, ctx_tiles, last_tile, final_norm):
    j = pl.program_id(1) + tile0
    has_prev = jnp.logical_and(j > 0, j != ctx_tiles)
    has_next = jnp.logical_and(j < last_tile, j != ctx_tiles - 1)
    ns, tm = x_ref.shape[0], x_ref.shape[1]
    for s in range(ns):
        _conv_prepare(yp_ref[s], yc_ref[s], yn_ref[s], has_prev, has_next, ybuf.at[s], ysh.at[s])
        mixb = jnp.concatenate(
            [_conv_ln_silu_rows(r0, ybuf.at[s], ysh.at[s], cw_ref, cb_ref, lng_ref, lnb_ref)
             for r0 in range(0, tm, CONV_ROWS)], axis=0)
        a = _dot(attn_ref[s], wo_ref[:MIX_HALF, :]) + _dot(mixb, wo_ref[MIX_HALF:, :])
        x1 = x_ref[s] + mod_ref[s, 2:3, :] * a
        h2 = _modulate(x1, g2_ref[...], mod_ref[s, 3:4, :], mod_ref[s, 4:5, :]).astype(BF16)
        out = x1 + mod_ref[s, 5:6, :] * _mlp(h2, w1_ref, w2_ref)
        if final_norm:
            out = _rms(out) * fg_ref[...]
        o_ref[s] = out


def _out_specs(ns, b, d, layer, ctx_tiles, tile0):
    half_spec = pl.BlockSpec((ns, TM, MIX_HALF), lambda i, j: (i, j, 0))
    mod_spec = pl.BlockSpec((None, ns, N_MOD, d),
                            lambda i, j: (layer, jnp.where(j + tile0 < ctx_tiles, b // ns, i), 0, 0))
    out_spec = pl.BlockSpec((ns, TM, d), lambda i, j: (i, j, 0))
    params = pltpu.CompilerParams(
        dimension_semantics=("arbitrary", "arbitrary"), vmem_limit_bytes=VMEM_LIMIT)
    return half_spec, mod_spec, out_spec, params


def _even_out_call(ctx, x, attn_ctx, attn_lat, mixb, mod, layer, g2, wo, w1, w2):
    b, seq, d = x.shape
    ns = OUT_SAMPLES
    ctx_tiles = ctx.shape[1] // TM
    nt = ctx_tiles + seq // TM
    full = lambda a: pl.BlockSpec(a.shape, lambda i, j: (0,) * a.ndim)
    whole = pl.BlockSpec(memory_space=pltpu.VMEM)
    half_spec, mod_spec, out_spec, params = _out_specs(ns, b, d, layer, ctx_tiles, 0)
    return pl.pallas_call(
        functools.partial(_even_out_kernel, ctx_tiles=ctx_tiles),
        out_shape=jax.ShapeDtypeStruct((b, nt * TM, d), F32),
        grid=(b // ns, nt),
        in_specs=[pl.BlockSpec((ns, TM, d), lambda i, j: (i, jnp.minimum(j, ctx_tiles - 1), 0)),
                  pl.BlockSpec((ns, TM, d), lambda i, j: (i, jnp.maximum(j - ctx_tiles, 0), 0)),
                  pl.BlockSpec((ns, TM, MIX_HALF),
                               lambda i, j: (i, jnp.minimum(j, ctx_tiles - 1), 0)),
                  pl.BlockSpec((ns, TM, MIX_HALF),
                               lambda i, j: (i, jnp.maximum(j - ctx_tiles, 0), 0)),
                  half_spec, mod_spec, full(g2), whole, whole, whole],
        out_specs=out_spec, compiler_params=params, name="even_out_mlp",
    )(ctx, x, attn_ctx, attn_lat, mixb, mod, g2, wo, w1, w2)


def _odd_out_call(xs, attn, y, mod, layer, g2, wo, w1, w2, cw, cb, lng, lnb, fg, n_ctx_rows, tile0,
                  final_norm):
    b, t, d = xs.shape
    ns = OUT_SAMPLES
    nt = t // TM - tile0
    ctx_tiles = n_ctx_rows // TM
    full = lambda a: pl.BlockSpec(a.shape, lambda i, j: (0,) * a.ndim)
    whole = pl.BlockSpec(memory_space=pltpu.VMEM)
    half_spec, mod_spec, out_spec, params = _out_specs(ns, b, d, layer, ctx_tiles, tile0)
    hb = TM // HALO
    last_halo = t // HALO - 1
    kern = functools.partial(_odd_out_kernel, tile0=tile0, ctx_tiles=ctx_tiles,
                             last_tile=t // TM - 1, final_norm=final_norm)
    return pl.pallas_call(
        kern,
        out_shape=jax.ShapeDtypeStruct((b, nt * TM, d), F32),
        grid=(b // ns, nt),
        in_specs=[pl.BlockSpec((ns, TM, d), lambda i, j: (i, j + tile0, 0)),
                  half_spec,
                  pl.BlockSpec((ns, HALO, MIX_HALF),
                               lambda i, j: (i, jnp.maximum((j + tile0) * hb - 1, 0), 0)),
                  pl.BlockSpec((ns, TM, MIX_HALF), lambda i, j: (i, j + tile0, 0)),
                  pl.BlockSpec((ns, HALO, MIX_HALF),
                               lambda i, j: (i, jnp.minimum((j + tile0 + 1) * hb, last_halo), 0)),
                  mod_spec, full(g2), whole, whole, whole,
                  full(cw), full(cb), full(lng), full(lnb), full(fg)],
        out_specs=out_spec,
        scratch_shapes=[pltpu.VMEM((ns, TM + 2 * HALO, MIX_HALF), F32),
                        pltpu.VMEM((ns, SUBLANES - 1, TM + 2 * HALO - SUBLANES, MIX_HALF), F32)],
        compiler_params=params, name="odd_out_mlp",
    )(xs, attn, y, y, y, mod, g2, wo, w1, w2, cw, cb, lng, lnb, fg)


def _axial_angles(length, d_rot):
    rows = length // GRID_W
    row = np.repeat(np.arange(rows), GRID_W).astype(np.float64)
    col = np.tile(np.arange(GRID_W), rows).astype(np.float64)
    d_axis = d_rot // 2
    inv = ROPE_THETA ** (-np.arange(0, d_axis, 2, dtype=np.float64) / d_axis)
    return np.concatenate([row[:, None] * inv, col[:, None] * inv], axis=-1)


def _rope_tables(seq, n_ctx, d_rot, lane0, period):
    ang = _axial_angles(seq, d_rot)
    cos = np.repeat(np.cos(ang), 2, axis=-1)
    sin = np.repeat(np.sin(ang), 2, axis=-1) * np.tile(np.array([-1.0, 1.0]), d_rot // 2)
    cos_p = np.ones((seq, period))
    sin_p = np.zeros((seq, period))
    cos_p[:, lane0:lane0 + d_rot] = cos
    sin_p[:, lane0:lane0 + d_rot] = sin
    cos_p = np.tile(cos_p, (1, LANES // period))
    sin_p = np.tile(sin_p, (1, LANES // period))
    cos_t = np.concatenate([np.ones((n_ctx, LANES)), cos_p], axis=0)
    sin_t = np.concatenate([np.zeros((n_ctx, LANES)), sin_p], axis=0)
    return jnp.asarray(cos_t, F32), jnp.asarray(sin_t, F32)


def _even_weights(w_in, q_g, k_g, sgu_g, sgu_w, sgu_b):
    d = w_in.shape[0]
    ev_q = A_Q_HEADS * A_HEAD_DIM
    ev_kv = A_KV_HEADS * A_HEAD_DIM
    wq = w_in[:, :ev_q].reshape(d, A_Q_HEADS, A_HEAD_DIM)
    assert A_KV_HEADS == 2
    order = [h for p in range(A_GROUP) for h in (p, p + A_GROUP)]
    w = jnp.concatenate([wq[:, h] for h in order] + [w_in[:, ev_q:]], axis=1).astype(BF16)
    gq = jnp.tile(q_g, A_Q_HEADS)[None, :]
    gk = jnp.tile(k_g, A_KV_HEADS)[None, :]
    sgug = sgu_g.reshape(1, -1)
    sgub = jnp.repeat(sgu_b.T, B_GROUP_DIM, axis=1)
    return w, gq, gk, sgug, sgu_w.astype(BF16), sgub


def _odd_weights(w_in, w_uq, w_ukv):
    d = w_in.shape[0]
    c0 = C_Q_RANK + C_KV_RANK
    w_kr = w_in[:, c0:c0 + C_ROPE]
    kr_slab = jnp.zeros((d, LANES), w_in.dtype).at[:, C_NOPE:C_NOPE + C_ROPE].set(w_kr)
    w = jnp.concatenate([w_in[:, :c0], kr_slab, w_in[:, c0 + C_ROPE:]], axis=1).astype(BF16)
    uq = w_uq.reshape(C_Q_RANK, C_HEADS, C_NOPE + C_ROPE)
    uq = jnp.pad(uq, ((0, 0), (0, 0), (0, LANES - C_NOPE - C_ROPE))).reshape(C_Q_RANK, C_HEADS * LANES)
    ukv = w_ukv.reshape(C_KV_RANK, C_HEADS, C_NOPE + C_V)
    uk = jnp.pad(ukv[:, :, :C_NOPE], ((0, 0), (0, 0), (0, LANES - C_NOPE))).reshape(C_KV_RANK, -1)
    uv = jnp.pad(ukv[:, :, C_NOPE:], ((0, 0), (0, 0), (0, LANES - C_V))).reshape(C_KV_RANK, -1)
    return w, uq.astype(BF16), jnp.concatenate([uk, uv], axis=1).astype(BF16)


def _group_matrix(group_lanes):
    idx = np.arange(2 * LANES) // group_lanes
    return jnp.asarray((idx[:, None] == idx[None, :]).astype(np.float32) / A_HEAD_DIM, BF16)


def kernel(x, c, ctx, c_ctx, ada_w, ada_b, norm1_g, norm2_g, w_out, mlp_w1, mlp_w2, ev_w_in, ev_q_norm_g, ev_k_norm_g, ev_sgu_norm_g, ev_sgu_w, ev_sgu_b, od_w_in, od_q_norm_g, od_kv_norm_g, od_w_uq, od_w_ukv, od_conv_w, od_conv_b, od_ln_g, od_ln_b, final_g):
    b, seq, d = x.shape
    n_ctx = ctx.shape[1]
    depth = ada_w.shape[0]
    assert d == D_MODEL and seq % TM == 0 and n_ctx % TM == 0 and depth == 2
    ctx_tiles = n_ctx // TM

    n_rep = max(IN_SAMPLES, OUT_SAMPLES)
    assert b % IN_SAMPLES == 0 and b % OUT_SAMPLES == 0
    r = -(-(b + n_rep) // SUBLANES) * SUBLANES
    cvec = jnp.concatenate([c] + [c_ctx[None, :]] * n_rep
                           + [jnp.zeros((r - b - n_rep, d), F32)], axis=0)
    mod = _ada_call(cvec, ada_w, ada_b).reshape(depth, r, N_MOD, d)
    fg = final_g[None, :]

    wo = w_out.astype(BF16)
    w1 = mlp_w1.astype(BF16)
    w2 = mlp_w2.astype(BF16)

    cos_e, sin_e = _rope_tables(seq, n_ctx, A_HEAD_DIM, 0, A_HEAD_DIM)
    w_e, gq, gk, sgug, sguw, sgub = _even_weights(ev_w_in[0], ev_q_norm_g[0], ev_k_norm_g[0],
                                                  ev_sgu_norm_g[0], ev_sgu_w[0], ev_sgu_b[0])
    q, k, v, mixb = _even_in_call(ctx, x, mod, 0, norm1_g[0][None, :], w_e, gq, gk, cos_e, sin_e,
                                  _group_matrix(A_HEAD_DIM), sgug, sguw, sgub)
    groups = tuple((h % A_GROUP, h // A_GROUP, 0, h // A_GROUP) for h in range(A_Q_HEADS))
    attn_lat = _attn_call(q, k, v, seq, groups, "even_attention")
    attn_ctx = _ctx_attn_call(q, k, v, seq, n_ctx, groups, "even_ctx_attention")
    xs = _even_out_call(ctx, x, attn_ctx, attn_lat, mixb, mod, 0, norm2_g[0][None, :], wo[0], w1[0],
                        w2[0])

    cos_o, sin_o = _rope_tables(seq, n_ctx, C_ROPE, C_NOPE, LANES)
    w_o, uq, ukv = _odd_weights(od_w_in[0], od_w_uq[0], od_w_ukv[0])
    q, k, v, y = _odd_in_call(xs, mod, 1, norm1_g[1][None, :], w_o, od_q_norm_g[0][None, :],
                              od_kv_norm_g[0][None, :], uq, ukv, cos_o, sin_o, n_ctx)
    groups = tuple((h, None, h, h) for h in range(C_HEADS))
    cw = jnp.broadcast_to(od_conv_w[0][:, None, :], (D_CONV, SUBLANES, MIX_HALF))
    attn = _attn_call(q, k, v, seq, groups, "odd_attention")
    return _odd_out_call(xs, attn, y, mod, 1, norm2_g[1][None, :], wo[1], w1[1], w2[1], cw,
                         od_conv_b[0][None, :], od_ln_g[0][None, :], od_ln_b[0][None, :], fg, n_ctx,
                         ctx_tiles, True)
```

```python
import functools

import numpy as np
import jax
import jax.numpy as jnp
from jax import lax
from jax.experimental import pallas as pl
from jax.experimental.pallas import tpu as pltpu

F32 = jnp.float32
BF16 = jnp.bfloat16

D_MODEL = 1024
GRID_W = 64
ROPE_THETA = 10000.0
EPS = 1e-6
MIX_HALF = D_MODEL // 2
N_MOD = 6

A_HEAD_DIM = 64
A_Q_HEADS = 8
A_KV_HEADS = 2
A_GROUP = A_Q_HEADS // A_KV_HEADS
B_GROUPS = 8
B_GROUP_DIM = 64
B_CHUNK = 128
C_HEADS = 8
C_NOPE = 64
C_ROPE = 32
C_V = 64
C_Q_RANK = 256
C_KV_RANK = 128
D_CONV = 31
FF_DIM = 4 * D_MODEL

LANES = 128
TM = 256
ADA_BLOCK_N = 1536
FF_CHUNK = 1024
SUBLANES = 8
ATT_TILES = 2
IN_SAMPLES = 4
OUT_SAMPLES = 2
CONV_ROWS = 32
HALO = 16
VMEM_LIMIT = 56 * 1024 * 1024
LOG2E = 1.4426950408889634


def _dot(a, b):
    return jnp.dot(a, b, preferred_element_type=F32)


def _dot_nt(a, b):
    return lax.dot_general(a, b, (((1,), (1,)), ((), ())), preferred_element_type=F32)


def _rms(x):
    return x * lax.rsqrt(jnp.mean(x * x, axis=-1, keepdims=True) + EPS)


def _modulate(x, g, shift, scale):
    return (_rms(x) * g) * (1.0 + scale) + shift


def _pair_swap(x):
    even = lax.broadcasted_iota(jnp.int32, (x.shape[0], LANES), 1) % 2 == 0
    slabs = []
    for s in range(0, x.shape[-1], LANES):
        xs = x[:, s:s + LANES]
        slabs.append(jnp.where(even, pltpu.roll(xs, LANES - 1, 1), pltpu.roll(xs, 1, 1)))
    return slabs[0] if len(slabs) == 1 else jnp.concatenate(slabs, axis=-1)


def _rope(x, cos, sin_signed):
    reps = x.shape[-1] // LANES
    if reps > 1:
        cos = jnp.concatenate([cos] * reps, axis=-1)
        sin_signed = jnp.concatenate([sin_signed] * reps, axis=-1)
    return x * cos + _pair_swap(x) * sin_signed


def _q_tile_index(ctx_tiles, nt):
    return lambda i, j: (i, jnp.where(j < ctx_tiles, j + nt - ctx_tiles, j - ctx_tiles), 0)


def _group_mean_sq(x, gmat):
    w = gmat.shape[0]
    x2 = (x * x).astype(BF16)
    parts = [_dot(x2[:, i:i + w], gmat) for i in range(0, x.shape[-1], w)]
    return parts[0] if len(parts) == 1 else jnp.concatenate(parts, axis=-1)


def _ada_kernel(c_ref, w_ref, b_ref, o_ref):
    c = c_ref[...]
    s = c * (1.0 / (1.0 + jnp.exp(-c)))
    o_ref[...] = _dot(s.astype(BF16), w_ref[...].astype(BF16)) + b_ref[...]


def _ada_call(cvec, ada_w, ada_b):
    depth, d, n = ada_w.shape
    r = cvec.shape[0]
    bn = ADA_BLOCK_N
    assert n % bn == 0
    return pl.pallas_call(
        _ada_kernel,
        out_shape=jax.ShapeDtypeStruct((depth, r, n), F32),
        grid=(depth, n // bn),
        in_specs=[
            pl.BlockSpec((r, d), lambda i, j: (0, 0)),
            pl.BlockSpec((None, d, bn), lambda i, j: (i, 0, j)),
            pl.BlockSpec((None, 1, bn), lambda i, j: (i, 0, j)),
        ],
        out_specs=pl.BlockSpec((None, r, bn), lambda i, j: (i, 0, j)),
        compiler_params=pltpu.CompilerParams(
            dimension_semantics=("arbitrary", "arbitrary"), vmem_limit_bytes=VMEM_LIMIT),
        name="ada_mod",
    )(cvec, ada_w, ada_b.reshape(depth, 1, n))


def _even_in_kernel(ctx_ref, x_ref, mod_ref, g1_ref, w_ref, gq_ref, gk_ref, cos_ref, sin_ref,
                    gmat64_ref, sgug_ref, sguw_ref, sgub_ref, q_ref, k_ref, v_ref, mb_ref,
                    *, ctx_tiles):
    for s in range(x_ref.shape[0]):
        _even_in_tile(ctx_ref.at[s], x_ref.at[s], mod_ref.at[s], g1_ref, w_ref, gq_ref, gk_ref,
                      cos_ref, sin_ref, gmat64_ref, sgug_ref, sguw_ref, sgub_ref,
                      q_ref.at[s], k_ref.at[s], v_ref.at[s], mb_ref.at[s], ctx_tiles)


def _even_in_tile(ctx_ref, x_ref, mod_ref, g1_ref, w_ref, gq_ref, gk_ref, cos_ref, sin_ref,
                  gmat64_ref, sgug_ref, sguw_ref, sgub_ref, q_ref, k_ref, v_ref, mb_ref,
                  ctx_tiles):
    x = jnp.where(pl.program_id(1) < ctx_tiles, ctx_ref[...], x_ref[...])
    h = _modulate(x, g1_ref[...], mod_ref[0:1, :], mod_ref[1:2, :])
    p = _dot(h.astype(BF16), w_ref[...])
    nq = A_Q_HEADS * A_HEAD_DIM
    q = p[:, :nq]
    k = p[:, nq:nq + LANES]
    v = p[:, nq + LANES:nq + 2 * LANES]
    z = p[:, nq + 2 * LANES:]
    gmat64 = gmat64_ref[...]
    cos = cos_ref[...]
    sin = sin_ref[...]

    qn = q * lax.rsqrt(_group_mean_sq(q, gmat64) + EPS) * gq_ref[...]
    q_ref[...] = (_rope(qn, cos, sin) * (A_HEAD_DIM ** -0.5 * LOG2E)).astype(BF16)
    kn = k * lax.rsqrt(_group_mean_sq(k, gmat64[:LANES, :LANES]) + EPS) * gk_ref[...]
    kr = _rope(kn, cos, sin)
    low = lax.broadcasted_iota(jnp.int32, v.shape, 1) < A_HEAD_DIM
    k_ref[:, :LANES] = jnp.where(low, kr, 0.0).astype(BF16)
    k_ref[:, LANES:] = jnp.where(low, 0.0, kr).astype(BF16)
    v_ref[:, :LANES] = jnp.where(low, v, 1.0).astype(BF16)
    v_ref[:, LANES:] = jnp.where(low, pltpu.roll(v, A_HEAD_DIM, 1), 1.0).astype(BF16)

    ge = 0.5 * z * (1.0 + jnp.tanh(np.sqrt(2.0 / np.pi).astype(np.float32)
                                   * (z + 0.044715 * (z * z * z))))
    u = ge[:, :MIX_HALF]
    vv = ge[:, MIX_HALF:]
    vn = vv * lax.rsqrt(_group_mean_sq(vv, gmat64) + EPS) * sgug_ref[...]
    vnb = vn.astype(BF16)
    lane = lax.broadcasted_iota(jnp.int32, (B_CHUNK, LANES), 1)
    low = lane < B_GROUP_DIM
    rows = []
    for c in range(x.shape[0] // B_CHUNK):
        slabs = []
        for s in range(MIX_HALF // LANES):
            vs = vnb[c * B_CHUNK:(c + 1) * B_CHUNK, s * LANES:(s + 1) * LANES]
            r0 = _dot(sguw_ref[2 * s], vs)
            r1 = _dot(sguw_ref[2 * s + 1], vs)
            slabs.append(jnp.where(low, r0, r1))
        rows.append(jnp.concatenate(slabs, axis=-1) + sgub_ref[...])
    sv = jnp.concatenate(rows, axis=0)
    mb_ref[...] = (u * sv).astype(BF16)


def _even_in_call(ctx, x, mod, layer, g1, w_in, gq, gk, cos, sin, gmat64, sgug, sguw, sgub):
    b, seq, d = x.shape
    ns = IN_SAMPLES
    n_ctx_rows = ctx.shape[1]
    ctx_tiles = n_ctx_rows // TM
    t = n_ctx_rows + seq
    nt = t // TM
    nq = A_Q_HEADS * A_HEAD_DIM
    full = lambda a: pl.BlockSpec(a.shape, lambda i, j: (0,) * a.ndim)
    return pl.pallas_call(
        functools.partial(_even_in_kernel, ctx_tiles=ctx_tiles),
        out_shape=(
            jax.ShapeDtypeStruct((b, t, nq), BF16),
            jax.ShapeDtypeStruct((b, t, A_KV_HEADS * LANES), BF16),
            jax.ShapeDtypeStruct((b, t, A_KV_HEADS * LANES), BF16),
            jax.ShapeDtypeStruct((b, t, MIX_HALF), BF16),
        ),
        grid=(b // ns, nt),
        in_specs=[
            pl.BlockSpec((ns, TM, d), lambda i, j: (i, jnp.minimum(j, ctx_tiles - 1), 0)),
            pl.BlockSpec((ns, TM, d), lambda i, j: (i, jnp.maximum(j - ctx_tiles, 0), 0)),
            pl.BlockSpec((None, ns, N_MOD, d),
                         lambda i, j: (layer, jnp.where(j < ctx_tiles, b // ns, i), 0, 0)),
            full(g1),
            pl.BlockSpec(memory_space=pltpu.VMEM),
            full(gq), full(gk),
            pl.BlockSpec((TM, LANES), lambda i, j: (j, 0)),
            pl.BlockSpec((TM, LANES), lambda i, j: (j, 0)),
            full(gmat64), full(sgug), full(sguw), full(sgub),
        ],
        out_specs=(
            pl.BlockSpec((ns, TM, nq), _q_tile_index(ctx_tiles, nt)),
            pl.BlockSpec((ns, TM, A_KV_HEADS * LANES), lambda i, j: (i, j, 0)),
            pl.BlockSpec((ns, TM, A_KV_HEADS * LANES), lambda i, j: (i, j, 0)),
            pl.BlockSpec((ns, TM, MIX_HALF), lambda i, j: (i, j, 0)),
        ),
        compiler_params=pltpu.CompilerParams(
            dimension_semantics=("arbitrary", "arbitrary"), vmem_limit_bytes=VMEM_LIMIT),
        name="even_in_proj",
    )(ctx, x, mod, g1, w_in, gq, gk, cos, sin, gmat64, sgug, sguw, sgub)


def _attn_body(q_ref, k_ref, v_ref, o_ref, n_keys, groups):
    for r0 in range(0, q_ref.shape[0], TM):
        _attn_rows(q_ref.at[r0:r0 + TM, :], k_ref, v_ref, o_ref.at[r0:r0 + TM, :], n_keys, groups)


def _attn_rows(q_ref, k_ref, v_ref, o_ref, n_keys, groups):
    tm = q_ref.shape[0]
    half = LANES // 2
    low = lax.broadcasted_iota(jnp.int32, (tm, LANES), 1) < half
    res = []
    for h, (qs, ks, vs) in enumerate(groups):
        s = _dot_nt(q_ref[:, qs * LANES:(qs + 1) * LANES], k_ref[:n_keys, ks * LANES:(ks + 1) * LANES])
        p = jnp.exp2(s - jnp.max(s, axis=-1, keepdims=True))
        vp = vs // 2 * 2
        pv = _dot(p.astype(BF16), v_ref[:n_keys, vp * LANES:(vp + 2) * LANES])
        pv = pv[:, (vs - vp) * LANES:(vs - vp + 1) * LANES]
        sw = pltpu.roll(pv, half, 1)
        res.append(pv / sw if h % 2 == 0 else sw / pv)
    for pair in range(len(res) // 2):
        o_ref[:, pair * LANES:(pair + 1) * LANES] = jnp.where(
            low, res[2 * pair], res[2 * pair + 1]).astype(BF16)


def _conv_prepare(yp, yc, yn, has_prev, has_next, ybuf, ysh):
    tm = yc.shape[0]
    ybuf[0:HALO, :] = jnp.where(has_prev, yp, 0.0)
    ybuf[HALO:HALO + tm, :] = yc
    ybuf[HALO + tm:, :] = jnp.where(has_next, yn, 0.0)
    n_sh = ysh.shape[1]
    for s in range(1, SUBLANES):
        ysh[s - 1] = ybuf[s:s + n_sh, :]


def _conv_ln_silu_rows(r0, ybuf, ysh, cw_ref, cb_ref, lng_ref, lnb_ref):
    base = HALO - D_CONV // 2
    nch = ybuf.shape[1]
    acc = None
    for tap in range(D_CONV):
        a, s = divmod(base + tap, SUBLANES)
        lo = SUBLANES * a + r0
        win = ybuf[lo:lo + CONV_ROWS, :] if s == 0 else ysh[s - 1, lo:lo + CONV_ROWS, :]
        term = win.reshape(CONV_ROWS // SUBLANES, SUBLANES, nch) * cw_ref[tap]
        acc = term if acc is None else acc + term
    y = acc.reshape(CONV_ROWS, nch) + cb_ref[...]
    mu = jnp.mean(y, axis=-1, keepdims=True)
    yc = y - mu
    var = jnp.mean(yc * yc, axis=-1, keepdims=True)
    ln = yc * lax.rsqrt(var + EPS) * lng_ref[...] + lnb_ref[...]
    return (ln * (1.0 / (1.0 + jnp.exp(-ln)))).astype(BF16)


def _attn_kernel(q_ref, k_ref, v_ref, o_ref, *, groups):
    _attn_body(q_ref, k_ref, v_ref, o_ref, k_ref.shape[0], groups)


def _attn_call(q, k, v, seq, groups, name):
    b, t, nq = q.shape
    rows = ATT_TILES * TM
    return pl.pallas_call(
        functools.partial(_attn_kernel, groups=groups),
        out_shape=jax.ShapeDtypeStruct((b, seq, MIX_HALF), BF16),
        grid=(b, seq // rows),
        in_specs=[
            pl.BlockSpec((None, rows, nq), lambda i, j: (i, j, 0)),
            pl.BlockSpec((None, t, k.shape[-1]), lambda i, j: (i, 0, 0)),
            pl.BlockSpec((None, t, v.shape[-1]), lambda i, j: (i, 0, 0)),
        ],
        out_specs=pl.BlockSpec((None, rows, MIX_HALF), lambda i, j: (i, j, 0)),
        compiler_params=pltpu.CompilerParams(
            dimension_semantics=("arbitrary", "arbitrary"), vmem_limit_bytes=VMEM_LIMIT),
        name=name,
    )(q, k, v)


def _ctx_attn_call(q, k, v, seq, n_ctx, groups, name):
    b, t, nq = q.shape
    assert seq % n_ctx == 0
    return pl.pallas_call(
        functools.partial(_attn_kernel, groups=groups),
        out_shape=jax.ShapeDtypeStruct((b, n_ctx, MIX_HALF), BF16),
        grid=(b,),
        in_specs=[
            pl.BlockSpec((None, n_ctx, nq), lambda i: (i, seq // n_ctx, 0)),
            pl.BlockSpec((None, n_ctx, k.shape[-1]), lambda i: (i, 0, 0)),
            pl.BlockSpec((None, n_ctx, v.shape[-1]), lambda i: (i, 0, 0)),
        ],
        out_specs=pl.BlockSpec((None, n_ctx, MIX_HALF), lambda i: (i, 0, 0)),
        compiler_params=pltpu.CompilerParams(
            dimension_semantics=("arbitrary",), vmem_limit_bytes=VMEM_LIMIT),
        name=name,
    )(q, k, v)


def _odd_in_kernel(x_ref, mod_ref, g1_ref, w_ref, gcq_ref, gckv_ref, wuq_ref, wukv_ref,
                   cos_ref, sin_ref, q_ref, k_ref, v_ref, y_ref):
    for s in range(x_ref.shape[0]):
        _odd_in_tile(x_ref.at[s], mod_ref.at[s], g1_ref, w_ref, gcq_ref, gckv_ref, wuq_ref, wukv_ref,
                     cos_ref, sin_ref, q_ref.at[s], k_ref.at[s], v_ref.at[s], y_ref.at[s])


def _odd_in_tile(x_ref, mod_ref, g1_ref, w_ref, gcq_ref, gckv_ref, wuq_ref, wukv_ref,
                 cos_ref, sin_ref, q_ref, k_ref, v_ref, y_ref):
    x = x_ref[...]
    h = _modulate(x, g1_ref[...], mod_ref[0:1, :], mod_ref[1:2, :])
    p = _dot(h.astype(BF16), w_ref[...])
    cq = p[:, :C_Q_RANK]
    ckv = p[:, C_Q_RANK:C_Q_RANK + C_KV_RANK]
    o = C_Q_RANK + C_KV_RANK
    kr = p[:, o:o + LANES]
    z = p[:, o + LANES:]
    cos = cos_ref[...]
    sin = sin_ref[...]

    q = _dot((_rms(cq) * gcq_ref[...]).astype(BF16), wuq_ref[...])
    q_ref[...] = (_rope(q, cos, sin) * ((C_NOPE + C_ROPE) ** -0.5 * LOG2E)).astype(BF16)
    kv = _dot((_rms(ckv) * gckv_ref[...]).astype(BF16), wukv_ref[...])
    nk = C_HEADS * LANES
    krr = _rope(kr, cos, sin)
    k_ref[...] = (kv[:, :nk] + jnp.concatenate([krr] * C_HEADS, axis=-1)).astype(BF16)
    v = kv[:, nk:]
    high = lax.broadcasted_iota(jnp.int32, v.shape, 1) % LANES >= C_V
    v_ref[...] = jnp.where(high, 1.0, v).astype(BF16)
    a = z[:, :MIX_HALF]
    gt = z[:, MIX_HALF:]
    y_ref[...] = a * (1.0 / (1.0 + jnp.exp(-gt)))


def _odd_in_call(xs, mod, layer, g1, w_in, gcq, gckv, wuq, wukv, cos, sin, n_ctx_rows):
    b, t, d = xs.shape
    ns = IN_SAMPLES
    nt = t // TM
    nk = C_HEADS * LANES
    full = lambda a: pl.BlockSpec(a.shape, lambda i, j: (0,) * a.ndim)
    return pl.pallas_call(
        _odd_in_kernel,
        out_shape=(
            jax.ShapeDtypeStruct((b, t, nk), BF16),
            jax.ShapeDtypeStruct((b, t, nk), BF16),
            jax.ShapeDtypeStruct((b, t, nk), BF16),
            jax.ShapeDtypeStruct((b, t, MIX_HALF), F32),
        ),
        grid=(b // ns, nt),
        in_specs=[
            pl.BlockSpec((ns, TM, d), lambda i, j: (i, j, 0)),
            pl.BlockSpec((None, ns, N_MOD, d),
                         lambda i, j: (layer, jnp.where(j < n_ctx_rows // TM, b // ns, i), 0, 0)),
            full(g1),
            pl.BlockSpec(memory_space=pltpu.VMEM),
            full(gcq), full(gckv),
            pl.BlockSpec(memory_space=pltpu.VMEM),
            pl.BlockSpec(memory_space=pltpu.VMEM),
            pl.BlockSpec((TM, LANES), lambda i, j: (j, 0)),
            pl.BlockSpec((TM, LANES), lambda i, j: (j, 0)),
        ],
        out_specs=(
            pl.BlockSpec((ns, TM, nk), _q_tile_index(n_ctx_rows // TM, nt)),
            pl.BlockSpec((ns, TM, nk), lambda i, j: (i, j, 0)),
            pl.BlockSpec((ns, TM, nk), lambda i, j: (i, j, 0)),
            pl.BlockSpec((ns, TM, MIX_HALF), lambda i, j: (i, j, 0)),
        ),
        compiler_params=pltpu.CompilerParams(
            dimension_semantics=("arbitrary", "arbitrary"), vmem_limit_bytes=VMEM_LIMIT),
        name="odd_in_proj",
    )(xs, mod, g1, w_in, gcq, gckv, wuq, wukv, cos, sin)


def _mlp(h2, w1_ref, w2_ref):
    acc = None
    for c in range(0, FF_DIM, FF_CHUNK):
        hc = jnp.maximum(_dot(h2, w1_ref[:, c:c + FF_CHUNK]), 0.0)
        part = _dot((hc * hc).astype(BF16), w2_ref[c:c + FF_CHUNK, :])
        acc = part if acc is None else acc + part
    return acc


def _even_out_kernel(ctx_ref, x_ref, actx_ref, alat_ref, mb_ref, mod_ref, g2_ref, wo_ref, w1_ref,
                     w2_ref, o_ref, *, ctx_tiles):
    ns, tm = x_ref.shape[0], x_ref.shape[1]
    is_ctx = pl.program_id(1) < ctx_tiles
    stack = lambda parts: jnp.concatenate(parts, axis=0)
    attn = jnp.where(is_ctx, stack([actx_ref[s] for s in range(ns)]),
                     stack([alat_ref[s] for s in range(ns)]))
    a = (_dot(attn, wo_ref[:MIX_HALF, :])
         + _dot(stack([mb_ref[s] for s in range(ns)]), wo_ref[MIX_HALF:, :]))
    x1 = [jnp.where(is_ctx, ctx_ref[s], x_ref[s]) + mod_ref[s, 2:3, :] * a[s * tm:(s + 1) * tm]
          for s in range(ns)]
    h2 = stack([_modulate(x1[s], g2_ref[...], mod_ref[s, 3:4, :], mod_ref[s, 4:5, :])
                for s in range(ns)]).astype(BF16)
    acc = _mlp(h2, w1_ref, w2_ref)
    for s in range(ns):
        o_ref[s] = x1[s] + mod_ref[s, 5:6, :] * acc[s * tm:(s + 1) * tm]


def _odd_out_kernel(x_ref, attn_ref, yp_ref, yc_ref, yn_ref, mod_ref, g2_ref, wo_ref, w1_ref, w2_ref,
                    cw_ref, cb_ref, lng_ref, lnb_ref, fg_ref, o_ref, ybuf, ysh, *,
                    tile0, ctx_tiles, last_tile, final_norm):
    j = pl.program_id(1) + tile0
    has_prev = jnp.logical_and(j > 0, j != ctx_tiles)
    has_next = jnp.logical_and(j < last_tile, j != ctx_tiles - 1)
    ns, tm = x_ref.shape[0], x_ref.shape[1]
    for s in range(ns):
        _conv_prepare(yp_ref[s], yc_ref[s], yn_ref[s], has_prev, has_next, ybuf.at[s], ysh.at[s])
        mixb = jnp.concatenate(
            [_conv_ln_silu_rows(r0, ybuf.at[s], ysh.at[s], cw_ref, cb_ref, lng_ref, lnb_ref)
             for r0 in range(0, tm, CONV_ROWS)], axis=0)
        a = _dot(attn_ref[s], wo_ref[:MIX_HALF, :]) + _dot(mixb, wo_ref[MIX_HALF:, :])
        x1 = x_ref[s] + mod_ref[s, 2:3, :] * a
        h2 = _modulate(x1, g2_ref[...], mod_ref[s, 3:4, :], mod_ref[s, 4:5, :]).astype(BF16)
        out = x1 + mod_ref[s, 5:6, :] * _mlp(h2, w1_ref, w2_ref)
        if final_norm:
            out = _rms(out) * fg_ref[...]
        o_ref[s] = out


def _out_specs(ns, b, d, layer, ctx_tiles, tile0):
    half_spec = pl.BlockSpec((ns, TM, MIX_HALF), lambda i, j: (i, j, 0))
    mod_spec = pl.BlockSpec((None, ns, N_MOD, d),
                            lambda i, j: (layer, jnp.where(j + tile0 < ctx_tiles, b // ns, i), 0, 0))
    out_spec = pl.BlockSpec((ns, TM, d), lambda i, j: (i, j, 0))
    params = pltpu.CompilerParams(
        dimension_semantics=("arbitrary", "arbitrary"), vmem_limit_bytes=VMEM_LIMIT)
    return half_spec, mod_spec, out_spec, params


def _even_out_call(ctx, x, attn_ctx, attn_lat, mixb, mod, layer, g2, wo, w1, w2):
    b, seq, d = x.shape
    ns = OUT_SAMPLES
    ctx_tiles = ctx.shape[1] // TM
    nt = ctx_tiles + seq // TM
    full = lambda a: pl.BlockSpec(a.shape, lambda i, j: (0,) * a.ndim)
    whole = pl.BlockSpec(memory_space=pltpu.VMEM)
    half_spec, mod_spec, out_spec, params = _out_specs(ns, b, d, layer, ctx_tiles, 0)
    return pl.pallas_call(
        functools.partial(_even_out_kernel, ctx_tiles=ctx_tiles),
        out_shape=jax.ShapeDtypeStruct((b, nt * TM, d), F32),
        grid=(b // ns, nt),
        in_specs=[pl.BlockSpec((ns, TM, d), lambda i, j: (i, jnp.minimum(j, ctx_tiles - 1), 0)),
                  pl.BlockSpec((ns, TM, d), lambda i, j: (i, jnp.maximum(j - ctx_tiles, 0), 0)),
                  pl.BlockSpec((ns, TM, MIX_HALF),
                               lambda i, j: (i, jnp.minimum(j, ctx_tiles - 1), 0)),
                  pl.BlockSpec((ns, TM, MIX_HALF),
                               lambda i, j: (i, jnp.maximum(j - ctx_tiles, 0), 0)),
                  half_spec, mod_spec, full(g2), whole, whole, whole],
        out_specs=out_spec, compiler_params=params, name="even_out_mlp",
    )(ctx, x, attn_ctx, attn_lat, mixb, mod, g2, wo, w1, w2)


def _odd_out_call(xs, attn, y, mod, layer, g2, wo, w1, w2, cw, cb, lng, lnb, fg, n_ctx_rows, tile0,
                  final_norm):
    b, t, d = xs.shape
    ns = OUT_SAMPLES
    nt = t // TM - tile0
    ctx_tiles = n_ctx_rows // TM
    full = lambda a: pl.BlockSpec(a.shape, lambda i, j: (0,) * a.ndim)
    whole = pl.BlockSpec(memory_space=pltpu.VMEM)
    half_spec, mod_spec, out_spec, params = _out_specs(ns, b, d, layer, ctx_tiles, tile0)
    hb = TM // HALO
    last_halo = t // HALO - 1
    kern = functools.partial(_odd_out_kernel, tile0=tile0, ctx_tiles=ctx_tiles,
                             last_tile=t // TM - 1, final_norm=final_norm)
    return pl.pallas_call(
        kern,
        out_shape=jax.ShapeDtypeStruct((b, nt * TM, d), F32),
        grid=(b // ns, nt),
        in_specs=[pl.BlockSpec((ns, TM, d), lambda i, j: (i, j + tile0, 0)),
                  half_spec,
                  pl.BlockSpec((ns, HALO, MIX_HALF),
                               lambda i, j: (i, jnp.maximum((j + tile0) * hb - 1, 0), 0)),
                  pl.BlockSpec((ns, TM, MIX_HALF), lambda i, j: (i, j + tile0, 0)),
                  pl.BlockSpec((ns, HALO, MIX_HALF),
                               lambda i, j: (i, jnp.minimum((j + tile0 + 1) * hb, last_halo), 0)),
                  mod_spec, full(g2), whole, whole, whole,
                  full(cw), full(cb), full(lng), full(lnb), full(fg)],
        out_specs=out_spec,
        scratch_shapes=[pltpu.VMEM((ns, TM + 2 * HALO, MIX_HALF), F32),
                        pltpu.VMEM((ns, SUBLANES - 1, TM + 2 * HALO - SUBLANES, MIX_HALF), F32)],
        compiler_params=params, name="odd_out_mlp",
    )(xs, attn, y, y, y, mod, g2, wo, w1, w2, cw, cb, lng, lnb, fg)


def _axial_angles(length, d_rot):
    rows = length // GRID_W
    row = np.repeat(np.arange(rows), GRID_W).astype(np.float64)
    col = np.tile(np.arange(GRID_W), rows).astype(np.float64)
    d_axis = d_rot // 2
    inv = ROPE_THETA ** (-np.arange(0, d_axis, 2, dtype=np.float64) / d_axis)
    return np.concatenate([row[:, None] * inv, col[:, None] * inv], axis=-1)


def _rope_tables(seq, n_ctx, d_rot, lane0, period):
    ang = _axial_angles(seq, d_rot)
    cos = np.repeat(np.cos(ang), 2, axis=-1)
    sin = np.repeat(np.sin(ang), 2, axis=-1) * np.tile(np.array([-1.0, 1.0]), d_rot // 2)
    cos_p = np.ones((seq, period))
    sin_p = np.zeros((seq, period))
    cos_p[:, lane0:lane0 + d_rot] = cos
    sin_p[:, lane0:lane0 + d_rot] = sin
    cos_p = np.tile(cos_p, (1, LANES // period))
    sin_p = np.tile(sin_p, (1, LANES // period))
    cos_t = np.concatenate([np.ones((n_ctx, LANES)), cos_p], axis=0)
    sin_t = np.concatenate([np.zeros((n_ctx, LANES)), sin_p], axis=0)
    return jnp.asarray(cos_t, F32), jnp.asarray(sin_t, F32)


def _even_weights(w_in, q_g, k_g, sgu_g, sgu_w, sgu_b):
    d = w_in.shape[0]
    w_in = w_in.astype(BF16)
    ev_q = A_Q_HEADS * A_HEAD_DIM
    wq = w_in[:, :ev_q].reshape(d, A_Q_HEADS, A_HEAD_DIM)
    assert A_KV_HEADS == 2
    order = [h for p in range(A_GROUP) for h in (p, p + A_GROUP)]
    w = jnp.concatenate([wq[:, h] for h in order] + [w_in[:, ev_q:]], axis=1)
    gq = jnp.tile(q_g, A_Q_HEADS)[None, :]
    gk = jnp.tile(k_g, A_KV_HEADS)[None, :]
    sgug = sgu_g.reshape(1, -1)
    sgub = jnp.repeat(sgu_b.T, B_GROUP_DIM, axis=1)
    return w, gq, gk, sgug, sgu_w.astype(BF16), sgub


def _odd_weights(w_in, w_uq, w_ukv):
    d = w_in.shape[0]
    w_in, w_uq, w_ukv = w_in.astype(BF16), w_uq.astype(BF16), w_ukv.astype(BF16)
    c0 = C_Q_RANK + C_KV_RANK
    w_kr = w_in[:, c0:c0 + C_ROPE]
    kr_slab = jnp.pad(w_kr, ((0, 0), (C_NOPE, LANES - C_NOPE - C_ROPE)))
    w = jnp.concatenate([w_in[:, :c0], kr_slab, w_in[:, c0 + C_ROPE:]], axis=1)
    uq = w_uq.reshape(C_Q_RANK, C_HEADS, C_NOPE + C_ROPE)
    uq = jnp.pad(uq, ((0, 0), (0, 0), (0, LANES - C_NOPE - C_ROPE))).reshape(C_Q_RANK, C_HEADS * LANES)
    ukv = w_ukv.reshape(C_KV_RANK, C_HEADS, C_NOPE + C_V)
    uk = jnp.pad(ukv[:, :, :C_NOPE], ((0, 0), (0, 0), (0, LANES - C_NOPE))).reshape(C_KV_RANK, -1)
    uv = jnp.pad(ukv[:, :, C_NOPE:], ((0, 0), (0, 0), (0, LANES - C_V))).reshape(C_KV_RANK, -1)
    return w, uq, jnp.concatenate([uk, uv], axis=1)


def _group_matrix(group_lanes):
    idx = np.arange(2 * LANES) // group_lanes
    return jnp.asarray((idx[:, None] == idx[None, :]).astype(np.float32) / A_HEAD_DIM, BF16)


def kernel(x, c, ctx, c_ctx, ada_w, ada_b, norm1_g, norm2_g, w_out, mlp_w1, mlp_w2, ev_w_in, ev_q_norm_g, ev_k_norm_g, ev_sgu_norm_g, ev_sgu_w, ev_sgu_b, od_w_in, od_q_norm_g, od_kv_norm_g, od_w_uq, od_w_ukv, od_conv_w, od_conv_b, od_ln_g, od_ln_b, final_g):
    b, seq, d = x.shape
    n_ctx = ctx.shape[1]
    depth = ada_w.shape[0]
    assert d == D_MODEL and seq % TM == 0 and n_ctx % TM == 0 and depth == 2
    ctx_tiles = n_ctx // TM

    n_rep = max(IN_SAMPLES, OUT_SAMPLES)
    assert b % IN_SAMPLES == 0 and b % OUT_SAMPLES == 0
    r = -(-(b + n_rep) // SUBLANES) * SUBLANES
    cvec = jnp.concatenate([c] + [c_ctx[None, :]] * n_rep
                           + [jnp.zeros((r - b - n_rep, d), F32)], axis=0)
    mod = _ada_call(cvec, ada_w, ada_b).reshape(depth, r, N_MOD, d)
    fg = final_g[None, :]

    wo = w_out.astype(BF16)
    w1 = mlp_w1.astype(BF16)
    w2 = mlp_w2.astype(BF16)

    cos_e, sin_e = _rope_tables(seq, n_ctx, A_HEAD_DIM, 0, A_HEAD_DIM)
    w_e, gq, gk, sgug, sguw, sgub = _even_weights(ev_w_in[0], ev_q_norm_g[0], ev_k_norm_g[0],
                                                  ev_sgu_norm_g[0], ev_sgu_w[0], ev_sgu_b[0])
    q, k, v, mixb = _even_in_call(ctx, x, mod, 0, norm1_g[0][None, :], w_e, gq, gk, cos_e, sin_e,
                                  _group_matrix(A_HEAD_DIM), sgug, sguw, sgub)
    groups = tuple((h % A_GROUP, h // A_GROUP, h // A_GROUP) for h in range(A_Q_HEADS))
    attn_lat = _attn_call(q, k, v, seq, groups, "even_attention")
    attn_ctx = _ctx_attn_call(q, k, v, seq, n_ctx, groups, "even_ctx_attention")
    xs = _even_out_call(ctx, x, attn_ctx, attn_lat, mixb, mod, 0, norm2_g[0][None, :], wo[0], w1[0],
                        w2[0])

    cos_o, sin_o = _rope_tables(seq, n_ctx, C_ROPE, C_NOPE, LANES)
    w_o, uq, ukv = _odd_weights(od_w_in[0], od_w_uq[0], od_w_ukv[0])
    q, k, v, y = _odd_in_call(xs, mod, 1, norm1_g[1][None, :], w_o, od_q_norm_g[0][None, :],
                              od_kv_norm_g[0][None, :], uq, ukv, cos_o, sin_o, n_ctx)
    groups = tuple((h, h, h) for h in range(C_HEADS))
    cw = jnp.broadcast_to(od_conv_w[0][:, None, :], (D_CONV, SUBLANES, MIX_HALF))
    attn = _attn_call(q, k, v, seq, groups, "odd_attention")
    return _odd_out_call(xs, attn, y, mod, 1, norm2_g[1][None, :], wo[1], w1[1], w2[1], cw,
                         od_conv_b[0][None, :], od_ln_g[0][None, :], od_ln_b[0][None, :], fg, n_ctx,
                         ctx_tiles, True)
```

```python
import functools

import numpy as np
import jax
import jax.numpy as jnp
from jax import lax
from jax.experimental import pallas as pl
from jax.experimental.pallas import tpu as pltpu

F32 = jnp.float32
BF16 = jnp.bfloat16

D_MODEL = 1024
GRID_W = 64
ROPE_THETA = 10000.0
EPS = 1e-6
MIX_HALF = D_MODEL // 2
N_MOD = 6

A_HEAD_DIM = 64
A_Q_HEADS = 8
A_KV_HEADS = 2
A_GROUP = A_Q_HEADS // A_KV_HEADS
B_GROUPS = 8
B_GROUP_DIM = 64
B_CHUNK = 128
C_HEADS = 8
C_NOPE = 64
C_ROPE = 32
C_V = 64
C_Q_RANK = 256
C_KV_RANK = 128
D_CONV = 31
FF_DIM = 4 * D_MODEL

LANES = 128
TM = 256
ADA_BLOCK_N = 1536
FF_CHUNK = 1024
SUBLANES = 8
ATT_TILES = 2
IN_SAMPLES = 4
OUT_SAMPLES = 2
CONV_ROWS = 32
HALO = 16
VMEM_LIMIT = 56 * 1024 * 1024
LOG2E = 1.4426950408889634


def _dot(a, b):
    return jnp.dot(a, b, preferred_element_type=F32)


def _dot_nt(a, b):
    return lax.dot_general(a, b, (((1,), (1,)), ((), ())), preferred_element_type=F32)


def _rms(x):
    return x * lax.rsqrt(jnp.mean(x * x, axis=-1, keepdims=True) + EPS)


def _modulate(x, g, shift, scale):
    return (_rms(x) * g) * (1.0 + scale) + shift


def _pair_swap(x):
    even = lax.broadcasted_iota(jnp.int32, (x.shape[0], LANES), 1) % 2 == 0
    slabs = []
    for s in range(0, x.shape[-1], LANES):
        xs = x[:, s:s + LANES]
        slabs.append(jnp.where(even, pltpu.roll(xs, LANES - 1, 1), pltpu.roll(xs, 1, 1)))
    return slabs[0] if len(slabs) == 1 else jnp.concatenate(slabs, axis=-1)


def _rope(x, cos, sin_signed):
    reps = x.shape[-1] // LANES
    if reps > 1:
        cos = jnp.concatenate([cos] * reps, axis=-1)
        sin_signed = jnp.concatenate([sin_signed] * reps, axis=-1)
    return x * cos + _pair_swap(x) * sin_signed


def _q_tile_index(ctx_tiles, nt):
    return lambda i, j: (i, jnp.where(j < ctx_tiles, j + nt - ctx_tiles, j - ctx_tiles), 0)


def _group_mean_sq(x, gmat):
    w = gmat.shape[0]
    x2 = (x * x).astype(BF16)
    parts = [_dot(x2[:, i:i + w], gmat) for i in range(0, x.shape[-1], w)]
    return parts[0] if len(parts) == 1 else jnp.concatenate(parts, axis=-1)


def _ada_kernel(c_ref, w_ref, b_ref, o_ref):
    c = c_ref[...]
    s = c * (1.0 / (1.0 + jnp.exp(-c)))
    o_ref[...] = _dot(s.astype(BF16), w_ref[...].astype(BF16)) + b_ref[...]


def _ada_call(cvec, ada_w, ada_b):
    depth, d, n = ada_w.shape
    r = cvec.shape[0]
    bn = ADA_BLOCK_N
    assert n % bn == 0
    return pl.pallas_call(
        _ada_kernel,
        out_shape=jax.ShapeDtypeStruct((depth, r, n), F32),
        grid=(depth, n // bn),
        in_specs=[
            pl.BlockSpec((r, d), lambda i, j: (0, 0)),
            pl.BlockSpec((None, d, bn), lambda i, j: (i, 0, j)),
            pl.BlockSpec((None, 1, bn), lambda i, j: (i, 0, j)),
        ],
        out_specs=pl.BlockSpec((None, r, bn), lambda i, j: (i, 0, j)),
        compiler_params=pltpu.CompilerParams(
            dimension_semantics=("arbitrary", "arbitrary"), vmem_limit_bytes=VMEM_LIMIT),
        name="ada_mod",
    )(cvec, ada_w, ada_b.reshape(depth, 1, n))


def _even_in_kernel(ctx_ref, x_ref, mod_ref, g1_ref, w_ref, gq_ref, gk_ref, cos_ref, sin_ref,
                    gmat64_ref, sgug_ref, sguw_ref, sgub_ref, q_ref, k_ref, v_ref, mb_ref,
                    *, ctx_tiles):
    for s in range(x_ref.shape[0]):
        _even_in_tile(ctx_ref.at[s], x_ref.at[s], mod_ref.at[s], g1_ref, w_ref, gq_ref, gk_ref,
                      cos_ref, sin_ref, gmat64_ref, sgug_ref, sguw_ref, sgub_ref,
                      q_ref.at[s], k_ref.at[s], v_ref.at[s], mb_ref.at[s], ctx_tiles)


def _even_in_tile(ctx_ref, x_ref, mod_ref, g1_ref, w_ref, gq_ref, gk_ref, cos_ref, sin_ref,
                  gmat64_ref, sgug_ref, sguw_ref, sgub_ref, q_ref, k_ref, v_ref, mb_ref,
                  ctx_tiles):
    x = jnp.where(pl.program_id(1) < ctx_tiles, ctx_ref[...], x_ref[...])
    h = _modulate(x, g1_ref[...], mod_ref[0:1, :], mod_ref[1:2, :])
    p = _dot(h.astype(BF16), w_ref[...])
    nq = A_Q_HEADS * A_HEAD_DIM
    q = p[:, :nq]
    k = p[:, nq:nq + LANES]
    v = p[:, nq + LANES:nq + 2 * LANES]
    z = p[:, nq + 2 * LANES:]
    gmat64 = gmat64_ref[...]
    cos = cos_ref[...]
    sin = sin_ref[...]

    qn = q * lax.rsqrt(_group_mean_sq(q, gmat64) + EPS) * gq_ref[...]
    q_ref[...] = (_rope(qn, cos, sin) * (A_HEAD_DIM ** -0.5 * LOG2E)).astype(BF16)
    kn = k * lax.rsqrt(_group_mean_sq(k, gmat64[:LANES, :LANES]) + EPS) * gk_ref[...]
    k_ref[...] = _rope(kn, cos, sin).astype(BF16)
    low = lax.broadcasted_iota(jnp.int32, v.shape, 1) < A_HEAD_DIM
    v_ref[:, :LANES] = jnp.where(low, v, 1.0).astype(BF16)
    v_ref[:, LANES:] = jnp.where(low, pltpu.roll(v, A_HEAD_DIM, 1), 1.0).astype(BF16)

    ge = 0.5 * z * (1.0 + jnp.tanh(np.sqrt(2.0 / np.pi).astype(np.float32)
                                   * (z + 0.044715 * (z * z * z))))
    u = ge[:, :MIX_HALF]
    vv = ge[:, MIX_HALF:]
    vn = vv * lax.rsqrt(_group_mean_sq(vv, gmat64) + EPS) * sgug_ref[...]
    vnb = vn.astype(BF16)
    lane = lax.broadcasted_iota(jnp.int32, (B_CHUNK, LANES), 1)
    low = lane < B_GROUP_DIM
    rows = []
    for c in range(x.shape[0] // B_CHUNK):
        slabs = []
        for s in range(MIX_HALF // LANES):
            vs = vnb[c * B_CHUNK:(c + 1) * B_CHUNK, s * LANES:(s + 1) * LANES]
            r0 = _dot(sguw_ref[2 * s], vs)
            r1 = _dot(sguw_ref[2 * s + 1], vs)
            slabs.append(jnp.where(low, r0, r1))
        rows.append(jnp.concatenate(slabs, axis=-1) + sgub_ref[...])
    sv = jnp.concatenate(rows, axis=0)
    mb_ref[...] = (u * sv).astype(BF16)


def _even_in_call(ctx, x, mod, layer, g1, w_in, gq, gk, cos, sin, gmat64, sgug, sguw, sgub):
    b, seq, d = x.shape
    ns = IN_SAMPLES
    n_ctx_rows = ctx.shape[1]
    ctx_tiles = n_ctx_rows // TM
    t = n_ctx_rows + seq
    nt = t // TM
    nq = A_Q_HEADS * A_HEAD_DIM
    full = lambda a: pl.BlockSpec(a.shape, lambda i, j: (0,) * a.ndim)
    return pl.pallas_call(
        functools.partial(_even_in_kernel, ctx_tiles=ctx_tiles),
        out_shape=(
            jax.ShapeDtypeStruct((b, t, nq), BF16),
            jax.ShapeDtypeStruct((b, t, LANES), BF16),
            jax.ShapeDtypeStruct((b, t, A_KV_HEADS * LANES), BF16),
            jax.ShapeDtypeStruct((b, t, MIX_HALF), BF16),
        ),
        grid=(b // ns, nt),
        in_specs=[
            pl.BlockSpec((ns, TM, d), lambda i, j: (i, jnp.minimum(j, ctx_tiles - 1), 0)),
            pl.BlockSpec((ns, TM, d), lambda i, j: (i, jnp.maximum(j - ctx_tiles, 0), 0)),
            pl.BlockSpec((None, ns, N_MOD, d),
                         lambda i, j: (layer, jnp.where(j < ctx_tiles, b // ns, i), 0, 0)),
            full(g1),
            pl.BlockSpec(memory_space=pltpu.VMEM),
            full(gq), full(gk),
            pl.BlockSpec((TM, LANES), lambda i, j: (j, 0)),
            pl.BlockSpec((TM, LANES), lambda i, j: (j, 0)),
            full(gmat64), full(sgug), full(sguw), full(sgub),
        ],
        out_specs=(
            pl.BlockSpec((ns, TM, nq), _q_tile_index(ctx_tiles, nt)),
            pl.BlockSpec((ns, TM, LANES), lambda i, j: (i, j, 0)),
            pl.BlockSpec((ns, TM, A_KV_HEADS * LANES), lambda i, j: (i, j, 0)),
            pl.BlockSpec((ns, TM, MIX_HALF), lambda i, j: (i, j, 0)),
        ),
        compiler_params=pltpu.CompilerParams(
            dimension_semantics=("arbitrary", "arbitrary"), vmem_limit_bytes=VMEM_LIMIT),
        name="even_in_proj",
    )(ctx, x, mod, g1, w_in, gq, gk, cos, sin, gmat64, sgug, sguw, sgub)


def _attn_body(q_ref, k_ref, v_ref, o_ref, n_keys, groups):
    for r0 in range(0, q_ref.shape[0], TM):
        _attn_rows(q_ref.at[r0:r0 + TM, :], k_ref, v_ref, o_ref.at[r0:r0 + TM, :], n_keys, groups)


def _attn_rows(q_ref, k_ref, v_ref, o_ref, n_keys, groups):
    tm = q_ref.shape[0]
    half = LANES // 2
    low = lax.broadcasted_iota(jnp.int32, (tm, LANES), 1) < half
    res = []
    for h, (qs, q_half, ks, vs) in enumerate(groups):
        q = q_ref[:, qs * LANES:(qs + 1) * LANES]
        if q_half is not None:
            q = jnp.where(low if q_half == 0 else jnp.logical_not(low), q.astype(F32), 0.0).astype(BF16)
        s = _dot_nt(q, k_ref[:n_keys, ks * LANES:(ks + 1) * LANES])
        p = jnp.exp2(s - jnp.max(s, axis=-1, keepdims=True))
        vp = vs // 2 * 2
        pv = _dot(p.astype(BF16), v_ref[:n_keys, vp * LANES:(vp + 2) * LANES])
        pv = pv[:, (vs - vp) * LANES:(vs - vp + 1) * LANES]
        sw = pltpu.roll(pv, half, 1)
        res.append(pv / sw if h % 2 == 0 else sw / pv)
    for pair in range(len(res) // 2):
        o_ref[:, pair * LANES:(pair + 1) * LANES] = jnp.where(
            low, res[2 * pair], res[2 * pair + 1]).astype(BF16)


def _conv_prepare(yp, yc, yn, has_prev, has_next, ybuf, ysh):
    tm = yc.shape[0]
    ybuf[0:HALO, :] = jnp.where(has_prev, yp, 0.0)
    ybuf[HALO:HALO + tm, :] = yc
    ybuf[HALO + tm:, :] = jnp.where(has_next, yn, 0.0)
    n_sh = ysh.shape[1]
    for s in range(1, SUBLANES):
        ysh[s - 1] = ybuf[s:s + n_sh, :]


def _conv_ln_silu_rows(r0, ybuf, ysh, cw_ref, cb_ref, lng_ref, lnb_ref):
    base = HALO - D_CONV // 2
    nch = ybuf.shape[1]
    acc = None
    for tap in range(D_CONV):
        a, s = divmod(base + tap, SUBLANES)
        lo = SUBLANES * a + r0
        win = ybuf[lo:lo + CONV_ROWS, :] if s == 0 else ysh[s - 1, lo:lo + CONV_ROWS, :]
        term = win.reshape(CONV_ROWS // SUBLANES, SUBLANES, nch) * cw_ref[tap]
        acc = term if acc is None else acc + term
    y = acc.reshape(CONV_ROWS, nch) + cb_ref[...]
    mu = jnp.mean(y, axis=-1, keepdims=True)
    yc = y - mu
    var = jnp.mean(yc * yc, axis=-1, keepdims=True)
    ln = yc * lax.rsqrt(var + EPS) * lng_ref[...] + lnb_ref[...]
    return (ln * (1.0 / (1.0 + jnp.exp(-ln)))).astype(BF16)


def _attn_kernel(q_ref, k_ref, v_ref, o_ref, *, groups):
    _attn_body(q_ref, k_ref, v_ref, o_ref, k_ref.shape[0], groups)


def _attn_call(q, k, v, seq, groups, name):
    b, t, nq = q.shape
    rows = ATT_TILES * TM
    return pl.pallas_call(
        functools.partial(_attn_kernel, groups=groups),
        out_shape=jax.ShapeDtypeStruct((b, seq, MIX_HALF), BF16),
        grid=(b, seq // rows),
        in_specs=[
            pl.BlockSpec((None, rows, nq), lambda i, j: (i, j, 0)),
            pl.BlockSpec((None, t, k.shape[-1]), lambda i, j: (i, 0, 0)),
            pl.BlockSpec((None, t, v.shape[-1]), lambda i, j: (i, 0, 0)),
        ],
        out_specs=pl.BlockSpec((None, rows, MIX_HALF), lambda i, j: (i, j, 0)),
        compiler_params=pltpu.CompilerParams(
            dimension_semantics=("arbitrary", "arbitrary"), vmem_limit_bytes=VMEM_LIMIT),
        name=name,
    )(q, k, v)


def _ctx_attn_call(q, k, v, seq, n_ctx, groups, name):
    b, t, nq = q.shape
    assert seq % n_ctx == 0
    return pl.pallas_call(
        functools.partial(_attn_kernel, groups=groups),
        out_shape=jax.ShapeDtypeStruct((b, n_ctx, MIX_HALF), BF16),
        grid=(b,),
        in_specs=[
            pl.BlockSpec((None, n_ctx, nq), lambda i: (i, seq // n_ctx, 0)),
            pl.BlockSpec((None, n_ctx, k.shape[-1]), lambda i: (i, 0, 0)),
            pl.BlockSpec((None, n_ctx, v.shape[-1]), lambda i: (i, 0, 0)),
        ],
        out_specs=pl.BlockSpec((None, n_ctx, MIX_HALF), lambda i: (i, 0, 0)),
        compiler_params=pltpu.CompilerParams(
            dimension_semantics=("arbitrary",), vmem_limit_bytes=VMEM_LIMIT),
        name=name,
    )(q, k, v)


def _odd_in_kernel(x_ref, mod_ref, g1_ref, w_ref, gcq_ref, gckv_ref, wuq_ref, wukv_ref,
                   cos_ref, sin_ref, q_ref, k_ref, v_ref, y_ref):
    for s in range(x_ref.shape[0]):
        _odd_in_tile(x_ref.at[s], mod_ref.at[s], g1_ref, w_ref, gcq_ref, gckv_ref, wuq_ref, wukv_ref,
                     cos_ref, sin_ref, q_ref.at[s], k_ref.at[s], v_ref.at[s], y_ref.at[s])


def _odd_in_tile(x_ref, mod_ref, g1_ref, w_ref, gcq_ref, gckv_ref, wuq_ref, wukv_ref,
                 cos_ref, sin_ref, q_ref, k_ref, v_ref, y_ref):
    x = x_ref[...]
    h = _modulate(x, g1_ref[...], mod_ref[0:1, :], mod_ref[1:2, :])
    p = _dot(h.astype(BF16), w_ref[...])
    cq = p[:, :C_Q_RANK]
    ckv = p[:, C_Q_RANK:C_Q_RANK + C_KV_RANK]
    o = C_Q_RANK + C_KV_RANK
    kr = p[:, o:o + LANES]
    z = p[:, o + LANES:]
    cos = cos_ref[...]
    sin = sin_ref[...]

    q = _dot((_rms(cq) * gcq_ref[...]).astype(BF16), wuq_ref[...])
    q_ref[...] = (_rope(q, cos, sin) * ((C_NOPE + C_ROPE) ** -0.5 * LOG2E)).astype(BF16)
    kv = _dot((_rms(ckv) * gckv_ref[...]).astype(BF16), wukv_ref[...])
    nk = C_HEADS * LANES
    krr = _rope(kr, cos, sin)
    k_ref[...] = (kv[:, :nk] + jnp.concatenate([krr] * C_HEADS, axis=-1)).astype(BF16)
    v = kv[:, nk:]
    high = lax.broadcasted_iota(jnp.int32, v.shape, 1) % LANES >= C_V
    v_ref[...] = jnp.where(high, 1.0, v).astype(BF16)
    a = z[:, :MIX_HALF]
    gt = z[:, MIX_HALF:]
    y_ref[...] = a * (1.0 / (1.0 + jnp.exp(-gt)))


def _odd_in_call(xs, mod, layer, g1, w_in, gcq, gckv, wuq, wukv, cos, sin, n_ctx_rows):
    b, t, d = xs.shape
    ns = IN_SAMPLES
    nt = t // TM
    nk = C_HEADS * LANES
    full = lambda a: pl.BlockSpec(a.shape, lambda i, j: (0,) * a.ndim)
    return pl.pallas_call(
        _odd_in_kernel,
        out_shape=(
            jax.ShapeDtypeStruct((b, t, nk), BF16),
            jax.ShapeDtypeStruct((b, t, nk), BF16),
            jax.ShapeDtypeStruct((b, t, nk), BF16),
            jax.ShapeDtypeStruct((b, t, MIX_HALF), F32),
        ),
        grid=(b // ns, nt),
        in_specs=[
            pl.BlockSpec((ns, TM, d), lambda i, j: (i, j, 0)),
            pl.BlockSpec((None, ns, N_MOD, d),
                         lambda i, j: (layer, jnp.where(j < n_ctx_rows // TM, b // ns, i), 0, 0)),
            full(g1),
            pl.BlockSpec(memory_space=pltpu.VMEM),
            full(gcq), full(gckv),
            pl.BlockSpec(memory_space=pltpu.VMEM),
            pl.BlockSpec(memory_space=pltpu.VMEM),
            pl.BlockSpec((TM, LANES), lambda i, j: (j, 0)),
            pl.BlockSpec((TM, LANES), lambda i, j: (j, 0)),
        ],
        out_specs=(
            pl.BlockSpec((ns, TM, nk), _q_tile_index(n_ctx_rows // TM, nt)),
            pl.BlockSpec((ns, TM, nk), lambda i, j: (i, j, 0)),
            pl.BlockSpec((ns, TM, nk), lambda i, j: (i, j, 0)),
            pl.BlockSpec((ns, TM, MIX_HALF), lambda i, j: (i, j, 0)),
        ),
        compiler_params=pltpu.CompilerParams(
            dimension_semantics=("arbitrary", "arbitrary"), vmem_limit_bytes=VMEM_LIMIT),
        name="odd_in_proj",
    )(xs, mod, g1, w_in, gcq, gckv, wuq, wukv, cos, sin)


def _mlp(h2, w1_ref, w2_ref):
    acc = None
    for c in range(0, FF_DIM, FF_CHUNK):
        hc = jnp.maximum(_dot(h2, w1_ref[:, c:c + FF_CHUNK]), 0.0)
        part = _dot((hc * hc).astype(BF16), w2_ref[c:c + FF_CHUNK, :])
        acc = part if acc is None else acc + part
    return acc


def _even_out_kernel(ctx_ref, x_ref, actx_ref, alat_ref, mb_ref, mod_ref, g2_ref, wo_ref, w1_ref,
                     w2_ref, o_ref, *, ctx_tiles):
    ns, tm = x_ref.shape[0], x_ref.shape[1]
    is_ctx = pl.program_id(1) < ctx_tiles
    stack = lambda parts: jnp.concatenate(parts, axis=0)
    attn = jnp.where(is_ctx, stack([actx_ref[s] for s in range(ns)]),
                     stack([alat_ref[s] for s in range(ns)]))
    a = (_dot(attn, wo_ref[:MIX_HALF, :])
         + _dot(stack([mb_ref[s] for s in range(ns)]), wo_ref[MIX_HALF:, :]))
    x1 = [jnp.where(is_ctx, ctx_ref[s], x_ref[s]) + mod_ref[s, 2:3, :] * a[s * tm:(s + 1) * tm]
          for s in range(ns)]
    h2 = stack([_modulate(x1[s], g2_ref[...], mod_ref[s, 3:4, :], mod_ref[s, 4:5, :])
                for s in range(ns)]).astype(BF16)
    acc = _mlp(h2, w1_ref, w2_ref)
    for s in range(ns):
        o_ref[s] = x1[s] + mod_ref[s, 5:6, :] * acc[s * tm:(s + 1) * tm]


def _odd_out_kernel(x_ref, attn_ref, yp_ref, yc_ref, yn_ref, mod_ref, g2_ref, wo_ref, w1_ref, w2_ref,
                    cw_ref, cb_ref, lng_ref, lnb_ref, fg_ref, o_ref, ybuf, ysh, *,
                    tile0, ctx_tiles, last_tile, final_norm):
    j = pl.program_id(1) + tile0
    has_prev = jnp.logical_and(j > 0, j != ctx_tiles)
    has_next = jnp.logical_and(j < last_tile, j != ctx_tiles - 1)
    ns, tm = x_ref.shape[0], x_ref.shape[1]
    for s in range(ns):
        _conv_prepare(yp_ref[s], yc_ref[s], yn_ref[s], has_prev, has_next, ybuf.at[s], ysh.at[s])
        mixb = jnp.concatenate(
            [_conv_ln_silu_rows(r0, ybuf.at[s], ysh.at[s], cw_ref, cb_ref, lng_ref, lnb_ref)
             for r0 in range(0, tm, CONV_ROWS)], axis=0)
        a = _dot(attn_ref[s], wo_ref[:MIX_HALF, :]) + _dot(mixb, wo_ref[MIX_HALF:, :])
        x1 = x_ref[s] + mod_ref[s, 2:3, :] * a
        h2 = _modulate(x1, g2_ref[...], mod_ref[s, 3:4, :], mod_ref[s, 4:5, :]).astype(BF16)
        out = x1 + mod_ref[s, 5:6, :] * _mlp(h2, w1_ref, w2_ref)
        if final_norm:
            out = _rms(out) * fg_ref[...]
        o_ref[s] = out


def _out_specs(ns, b, d, layer, ctx_tiles, tile0):
    half_spec = pl.BlockSpec((ns, TM, MIX_HALF), lambda i, j: (i, j, 0))
    mod_spec = pl.BlockSpec((None, ns, N_MOD, d),
                            lambda i, j: (layer, jnp.where(j + tile0 < ctx_tiles, b // ns, i), 0, 0))
    out_spec = pl.BlockSpec((ns, TM, d), lambda i, j: (i, j, 0))
    params = pltpu.CompilerParams(
        dimension_semantics=("arbitrary", "arbitrary"), vmem_limit_bytes=VMEM_LIMIT)
    return half_spec, mod_spec, out_spec, params


def _even_out_call(ctx, x, attn_ctx, attn_lat, mixb, mod, layer, g2, wo, w1, w2):
    b, seq, d = x.shape
    ns = OUT_SAMPLES
    ctx_tiles = ctx.shape[1] // TM
    nt = ctx_tiles + seq // TM
    full = lambda a: pl.BlockSpec(a.shape, lambda i, j: (0,) * a.ndim)
    whole = pl.BlockSpec(memory_space=pltpu.VMEM)
    half_spec, mod_spec, out_spec, params = _out_specs(ns, b, d, layer, ctx_tiles, 0)
    return pl.pallas_call(
        functools.partial(_even_out_kernel, ctx_tiles=ctx_tiles),
        out_shape=jax.ShapeDtypeStruct((b, nt * TM, d), F32),
        grid=(b // ns, nt),
        in_specs=[pl.BlockSpec((ns, TM, d), lambda i, j: (i, jnp.minimum(j, ctx_tiles - 1), 0)),
                  pl.BlockSpec((ns, TM, d), lambda i, j: (i, jnp.maximum(j - ctx_tiles, 0), 0)),
                  pl.BlockSpec((ns, TM, MIX_HALF),
                               lambda i, j: (i, jnp.minimum(j, ctx_tiles - 1), 0)),
                  pl.BlockSpec((ns, TM, MIX_HALF),
                               lambda i, j: (i, jnp.maximum(j - ctx_tiles, 0), 0)),
                  half_spec, mod_spec, full(g2), whole, whole, whole],
        out_specs=out_spec, compiler_params=params, name="even_out_mlp",
    )(ctx, x, attn_ctx, attn_lat, mixb, mod, g2, wo, w1, w2)


def _odd_out_call(xs, attn, y, mod, layer, g2, wo, w1, w2, cw, cb, lng, lnb, fg, n_ctx_rows, tile0,
                  final_norm):
    b, t, d = xs.shape
    ns = OUT_SAMPLES
    nt = t // TM - tile0
    ctx_tiles = n_ctx_rows // TM
    full = lambda a: pl.BlockSpec(a.shape, lambda i, j: (0,) * a.ndim)
    whole = pl.BlockSpec(memory_space=pltpu.VMEM)
    half_spec, mod_spec, out_spec, params = _out_specs(ns, b, d, layer, ctx_tiles, tile0)
    hb = TM // HALO
    last_halo = t // HALO - 1
    kern = functools.partial(_odd_out_kernel, tile0=tile0, ctx_tiles=ctx_tiles,
                             last_tile=t // TM - 1, final_norm=final_norm)
    return pl.pallas_call(
        kern,
        out_shape=jax.ShapeDtypeStruct((b, nt * TM, d), F32),
        grid=(b // ns, nt),
        in_specs=[pl.BlockSpec((ns, TM, d), lambda i, j: (i, j + tile0, 0)),
                  half_spec,
                  pl.BlockSpec((ns, HALO, MIX_HALF),
                               lambda i, j: (i, jnp.maximum((j + tile0) * hb - 1, 0), 0)),
                  pl.BlockSpec((ns, TM, MIX_HALF), lambda i, j: (i, j + tile0, 0)),
                  pl.BlockSpec((ns, HALO, MIX_HALF),
                               lambda i, j: (i, jnp.minimum((j + tile0 + 1) * hb, last_halo), 0)),
                  mod_spec, full(g2), whole, whole, whole,
                  full(cw), full(cb), full(lng), full(lnb), full(fg)],
        out_specs=out_spec,
        scratch_shapes=[pltpu.VMEM((ns, TM + 2 * HALO, MIX_HALF), F32),
                        pltpu.VMEM((ns, SUBLANES - 1, TM + 2 * HALO - SUBLANES, MIX_HALF), F32)],
        compiler_params=params, name="odd_out_mlp",
    )(xs, attn, y, y, y, mod, g2, wo, w1, w2, cw, cb, lng, lnb, fg)


def _axial_angles(length, d_rot):
    rows = length // GRID_W
    row = np.repeat(np.arange(rows), GRID_W).astype(np.float64)
    col = np.tile(np.arange(GRID_W), rows).astype(np.float64)
    d_axis = d_rot // 2
    inv = ROPE_THETA ** (-np.arange(0, d_axis, 2, dtype=np.float64) / d_axis)
    return np.concatenate([row[:, None] * inv, col[:, None] * inv], axis=-1)


def _rope_tables(seq, n_ctx, d_rot, lane0, period):
    ang = _axial_angles(seq, d_rot)
    cos = np.repeat(np.cos(ang), 2, axis=-1)
    sin = np.repeat(np.sin(ang), 2, axis=-1) * np.tile(np.array([-1.0, 1.0]), d_rot // 2)
    cos_p = np.ones((seq, period))
    sin_p = np.zeros((seq, period))
    cos_p[:, lane0:lane0 + d_rot] = cos
    sin_p[:, lane0:lane0 + d_rot] = sin
    cos_p = np.tile(cos_p, (1, LANES // period))
    sin_p = np.tile(sin_p, (1, LANES // period))
    cos_t = np.concatenate([np.ones((n_ctx, LANES)), cos_p], axis=0)
    sin_t = np.concatenate([np.zeros((n_ctx, LANES)), sin_p], axis=0)
    return jnp.asarray(cos_t, F32), jnp.asarray(sin_t, F32)


def _even_weights(w_in, q_g, k_g, sgu_g, sgu_w, sgu_b):
    d = w_in.shape[0]
    ev_q = A_Q_HEADS * A_HEAD_DIM
    ev_kv = A_KV_HEADS * A_HEAD_DIM
    wq = w_in[:, :ev_q].reshape(d, A_Q_HEADS, A_HEAD_DIM)
    assert A_KV_HEADS == 2
    order = [h for p in range(A_GROUP) for h in (p, p + A_GROUP)]
    w = jnp.concatenate([wq[:, h] for h in order] + [w_in[:, ev_q:]], axis=1).astype(BF16)
    gq = jnp.tile(q_g, A_Q_HEADS)[None, :]
    gk = jnp.tile(k_g, A_KV_HEADS)[None, :]
    sgug = sgu_g.reshape(1, -1)
    sgub = jnp.repeat(sgu_b.T, B_GROUP_DIM, axis=1)
    return w, gq, gk, sgug, sgu_w.astype(BF16), sgub


def _odd_weights(w_in, w_uq, w_ukv):
    d = w_in.shape[0]
    c0 = C_Q_RANK + C_KV_RANK
    w_kr = w_in[:, c0:c0 + C_ROPE]
    kr_slab = jnp.zeros((d, LANES), w_in.dtype).at[:, C_NOPE:C_NOPE + C_ROPE].set(w_kr)
    w = jnp.concatenate([w_in[:, :c0], kr_slab, w_in[:, c0 + C_ROPE:]], axis=1).astype(BF16)
    uq = w_uq.reshape(C_Q_RANK, C_HEADS, C_NOPE + C_ROPE)
    uq = jnp.pad(uq, ((0, 0), (0, 0), (0, LANES - C_NOPE - C_ROPE))).reshape(C_Q_RANK, C_HEADS * LANES)
    ukv = w_ukv.reshape(C_KV_RANK, C_HEADS, C_NOPE + C_V)
    uk = jnp.pad(ukv[:, :, :C_NOPE], ((0, 0), (0, 0), (0, LANES - C_NOPE))).reshape(C_KV_RANK, -1)
    uv = jnp.pad(ukv[:, :, C_NOPE:], ((0, 0), (0, 0), (0, LANES - C_V))).reshape(C_KV_RANK, -1)
    return w, uq.astype(BF16), jnp.concatenate([uk, uv], axis=1).astype(BF16)


def _group_matrix(group_lanes):
    idx = np.arange(2 * LANES) // group_lanes
    return jnp.asarray((idx[:, None] == idx[None, :]).astype(np.float32) / A_HEAD_DIM, BF16)


def kernel(x, c, ctx, c_ctx, ada_w, ada_b, norm1_g, norm2_g, w_out, mlp_w1, mlp_w2, ev_w_in, ev_q_norm_g, ev_k_norm_g, ev_sgu_norm_g, ev_sgu_w, ev_sgu_b, od_w_in, od_q_norm_g, od_kv_norm_g, od_w_uq, od_w_ukv, od_conv_w, od_conv_b, od_ln_g, od_ln_b, final_g):
    b, seq, d = x.shape
    n_ctx = ctx.shape[1]
    depth = ada_w.shape[0]
    assert d == D_MODEL and seq % TM == 0 and n_ctx % TM == 0 and depth == 2
    ctx_tiles = n_ctx // TM

    n_rep = max(IN_SAMPLES, OUT_SAMPLES)
    assert b % IN_SAMPLES == 0 and b % OUT_SAMPLES == 0
    r = -(-(b + n_rep) // SUBLANES) * SUBLANES
    cvec = jnp.concatenate([c] + [c_ctx[None, :]] * n_rep
                           + [jnp.zeros((r - b - n_rep, d), F32)], axis=0)
    mod = _ada_call(cvec, ada_w, ada_b).reshape(depth, r, N_MOD, d)
    fg = final_g[None, :]

    wo = [w_out[i].astype(BF16) for i in range(depth)]
    w1 = [mlp_w1[i].astype(BF16) for i in range(depth)]
    w2 = [mlp_w2[i].astype(BF16) for i in range(depth)]

    cos_e, sin_e = _rope_tables(seq, n_ctx, A_HEAD_DIM, 0, A_HEAD_DIM)
    w_e, gq, gk, sgug, sguw, sgub = _even_weights(ev_w_in[0], ev_q_norm_g[0], ev_k_norm_g[0],
                                                  ev_sgu_norm_g[0], ev_sgu_w[0], ev_sgu_b[0])
    q, k, v, mixb = _even_in_call(ctx, x, mod, 0, norm1_g[0][None, :], w_e, gq, gk, cos_e, sin_e,
                                  _group_matrix(A_HEAD_DIM), sgug, sguw, sgub)
    groups = tuple((h % A_GROUP, h // A_GROUP, 0, h // A_GROUP) for h in range(A_Q_HEADS))
    attn_lat = _attn_call(q, k, v, seq, groups, "even_attention")
    attn_ctx = _ctx_attn_call(q, k, v, seq, n_ctx, groups, "even_ctx_attention")
    xs = _even_out_call(ctx, x, attn_ctx, attn_lat, mixb, mod, 0, norm2_g[0][None, :], wo[0], w1[0],
                        w2[0])

    cos_o, sin_o = _rope_tables(seq, n_ctx, C_ROPE, C_NOPE, LANES)
    w_o, uq, ukv = _odd_weights(od_w_in[0], od_w_uq[0], od_w_ukv[0])
    q, k, v, y = _odd_in_call(xs, mod, 1, norm1_g[1][None, :], w_o, od_q_norm_g[0][None, :],
                              od_kv_norm_g[0][None, :], uq, ukv, cos_o, sin_o, n_ctx)
    groups = tuple((h, None, h, h) for h in range(C_HEADS))
    cw = jnp.broadcast_to(od_conv_w[0][:, None, :], (D_CONV, SUBLANES, MIX_HALF))
    attn = _attn_call(q, k, v, seq, groups, "odd_attention")
    return _odd_out_call(xs, attn, y, mod, 1, norm2_g[1][None, :], wo[1], w1[1], w2[1], cw,
                         od_conv_b[0][None, :], od_ln_g[0][None, :], od_ln_b[0][None, :], fg, n_ctx,
                         ctx_tiles, True)
```

```python
import functools

import numpy as np
import jax
import jax.numpy as jnp
from jax import lax
from jax.experimental import pallas as pl
from jax.experimental.pallas import tpu as pltpu

F32 = jnp.float32
BF16 = jnp.bfloat16

D_MODEL = 1024
GRID_W = 64
ROPE_THETA = 10000.0
EPS = 1e-6
MIX_HALF = D_MODEL // 2
N_MOD = 6

A_HEAD_DIM = 64
A_Q_HEADS = 8
A_KV_HEADS = 2
A_GROUP = A_Q_HEADS // A_KV_HEADS
B_GROUPS = 8
B_GROUP_DIM = 64
B_CHUNK = 128
C_HEADS = 8
C_NOPE = 64
C_ROPE = 32
C_V = 64
C_Q_RANK = 256
C_KV_RANK = 128
D_CONV = 31
FF_DIM = 4 * D_MODEL

LANES = 128
TM = 256
ADA_BLOCK_N = 1536
FF_CHUNK = 1024
SUBLANES = 8
ATT_TILES = 2
IN_SAMPLES = 4
OUT_SAMPLES = 2
CONV_ROWS = 32
HALO = 16
VMEM_LIMIT = 56 * 1024 * 1024
LOG2E = 1.4426950408889634


def _dot(a, b):
    return jnp.dot(a, b, preferred_element_type=F32)


def _dot_nt(a, b):
    return lax.dot_general(a, b, (((1,), (1,)), ((), ())), preferred_element_type=F32)


def _rms(x):
    return x * lax.rsqrt(jnp.mean(x * x, axis=-1, keepdims=True) + EPS)


def _modulate(x, g, shift, scale):
    return (_rms(x) * g) * (1.0 + scale) + shift


def _pair_swap(x):
    even = lax.broadcasted_iota(jnp.int32, (x.shape[0], LANES), 1) % 2 == 0
    slabs = []
    for s in range(0, x.shape[-1], LANES):
        xs = x[:, s:s + LANES]
        slabs.append(jnp.where(even, pltpu.roll(xs, LANES - 1, 1), pltpu.roll(xs, 1, 1)))
    return slabs[0] if len(slabs) == 1 else jnp.concatenate(slabs, axis=-1)


def _rope(x, cos, sin_signed):
    reps = x.shape[-1] // LANES
    if reps > 1:
        cos = jnp.concatenate([cos] * reps, axis=-1)
        sin_signed = jnp.concatenate([sin_signed] * reps, axis=-1)
    return x * cos + _pair_swap(x) * sin_signed


def _q_tile_index(ctx_tiles, nt):
    return lambda i, j: (i, jnp.where(j < ctx_tiles, j + nt - ctx_tiles, j - ctx_tiles), 0)


def _group_mean_sq(x, gmat):
    w = gmat.shape[0]
    x2 = (x * x).astype(BF16)
    parts = [_dot(x2[:, i:i + w], gmat) for i in range(0, x.shape[-1], w)]
    return parts[0] if len(parts) == 1 else jnp.concatenate(parts, axis=-1)


def _ada_kernel(c_ref, w_ref, b_ref, o_ref):
    c = c_ref[...]
    s = c * (1.0 / (1.0 + jnp.exp(-c)))
    o_ref[...] = _dot(s.astype(BF16), w_ref[...].astype(BF16)) + b_ref[...]


def _ada_call(cvec, ada_w, ada_b):
    depth, d, n = ada_w.shape
    r = cvec.shape[0]
    bn = ADA_BLOCK_N
    assert n % bn == 0
    return pl.pallas_call(
        _ada_kernel,
        out_shape=jax.ShapeDtypeStruct((depth, r, n), F32),
        grid=(depth, n // bn),
        in_specs=[
            pl.BlockSpec((r, d), lambda i, j: (0, 0)),
            pl.BlockSpec((None, d, bn), lambda i, j: (i, 0, j)),
            pl.BlockSpec((None, 1, bn), lambda i, j: (i, 0, j)),
        ],
        out_specs=pl.BlockSpec((None, r, bn), lambda i, j: (i, 0, j)),
        compiler_params=pltpu.CompilerParams(
            dimension_semantics=("arbitrary", "arbitrary"), vmem_limit_bytes=VMEM_LIMIT),
        name="ada_mod",
    )(cvec, ada_w, ada_b.reshape(depth, 1, n))


def _even_in_kernel(ctx_ref, x_ref, mod_ref, g1_ref, w_ref, gq_ref, gk_ref, cos_ref, sin_ref,
                    gmat64_ref, sgug_ref, sguw_ref, sgub_ref, q_ref, k_ref, v_ref, mb_ref,
                    *, ctx_tiles):
    for s in range(x_ref.shape[0]):
        _even_in_tile(ctx_ref.at[s], x_ref.at[s], mod_ref.at[s], g1_ref, w_ref, gq_ref, gk_ref,
                      cos_ref, sin_ref, gmat64_ref, sgug_ref, sguw_ref, sgub_ref,
                      q_ref.at[s], k_ref.at[s], v_ref.at[s], mb_ref.at[s], ctx_tiles)


def _even_in_tile(ctx_ref, x_ref, mod_ref, g1_ref, w_ref, gq_ref, gk_ref, cos_ref, sin_ref,
                  gmat64_ref, sgug_ref, sguw_ref, sgub_ref, q_ref, k_ref, v_ref, mb_ref,
                  ctx_tiles):
    x = jnp.where(pl.program_id(1) < ctx_tiles, ctx_ref[...], x_ref[...])
    h = _modulate(x, g1_ref[...], mod_ref[0:1, :], mod_ref[1:2, :])
    p = _dot(h.astype(BF16), w_ref[...])
    nq = A_Q_HEADS * A_HEAD_DIM
    q = p[:, :nq]
    k = p[:, nq:nq + LANES]
    v = p[:, nq + LANES:nq + 2 * LANES]
    z = p[:, nq + 2 * LANES:]
    gmat64 = gmat64_ref[...]
    cos = cos_ref[...]
    sin = sin_ref[...]

    qn = q * lax.rsqrt(_group_mean_sq(q, gmat64) + EPS) * gq_ref[...]
    q_ref[...] = (_rope(qn, cos, sin) * (A_HEAD_DIM ** -0.5 * LOG2E)).astype(BF16)
    kn = k * lax.rsqrt(_group_mean_sq(k, gmat64[:LANES, :LANES]) + EPS) * gk_ref[...]
    k_ref[...] = _rope(kn, cos, sin).astype(BF16)
    low = lax.broadcasted_iota(jnp.int32, v.shape, 1) < A_HEAD_DIM
    v_ref[:, :LANES] = jnp.where(low, v, 1.0).astype(BF16)
    v_ref[:, LANES:] = jnp.where(low, pltpu.roll(v, A_HEAD_DIM, 1), 1.0).astype(BF16)

    ge = 0.5 * z * (1.0 + jnp.tanh(np.sqrt(2.0 / np.pi).astype(np.float32)
                                   * (z + 0.044715 * (z * z * z))))
    u = ge[:, :MIX_HALF]
    vv = ge[:, MIX_HALF:]
    vn = vv * lax.rsqrt(_group_mean_sq(vv, gmat64) + EPS) * sgug_ref[...]
    vnb = vn.astype(BF16)
    lane = lax.broadcasted_iota(jnp.int32, (B_CHUNK, LANES), 1)
    low = lane < B_GROUP_DIM
    rows = []
    for c in range(x.shape[0] // B_CHUNK):
        slabs = []
        for s in range(MIX_HALF // LANES):
            vs = vnb[c * B_CHUNK:(c + 1) * B_CHUNK, s * LANES:(s + 1) * LANES]
            r0 = _dot(sguw_ref[2 * s], vs)
            r1 = _dot(sguw_ref[2 * s + 1], vs)
            slabs.append(jnp.where(low, r0, r1))
        rows.append(jnp.concatenate(slabs, axis=-1) + sgub_ref[...])
    sv = jnp.concatenate(rows, axis=0)
    mb_ref[...] = (u * sv).astype(BF16)


def _even_in_call(ctx, x, mod, layer, g1, w_in, gq, gk, cos, sin, gmat64, sgug, sguw, sgub):
    b, seq, d = x.shape
    ns = IN_SAMPLES
    n_ctx_rows = ctx.shape[1]
    ctx_tiles = n_ctx_rows // TM
    t = n_ctx_rows + seq
    nt = t // TM
    nq = A_Q_HEADS * A_HEAD_DIM
    full = lambda a: pl.BlockSpec(a.shape, lambda i, j: (0,) * a.ndim)
    return pl.pallas_call(
        functools.partial(_even_in_kernel, ctx_tiles=ctx_tiles),
        out_shape=(
            jax.ShapeDtypeStruct((b, t, nq), BF16),
            jax.ShapeDtypeStruct((b, t, LANES), BF16),
            jax.ShapeDtypeStruct((b, t, A_KV_HEADS * LANES), BF16),
            jax.ShapeDtypeStruct((b, t, MIX_HALF), BF16),
        ),
        grid=(b // ns, nt),
        in_specs=[
            pl.BlockSpec((ns, TM, d), lambda i, j: (i, jnp.minimum(j, ctx_tiles - 1), 0)),
            pl.BlockSpec((ns, TM, d), lambda i, j: (i, jnp.maximum(j - ctx_tiles, 0), 0)),
            pl.BlockSpec((None, ns, N_MOD, d),
                         lambda i, j: (layer, jnp.where(j < ctx_tiles, b // ns, i), 0, 0)),
            full(g1),
            pl.BlockSpec(memory_space=pltpu.VMEM),
            full(gq), full(gk),
            pl.BlockSpec((TM, LANES), lambda i, j: (j, 0)),
            pl.BlockSpec((TM, LANES), lambda i, j: (j, 0)),
            full(gmat64), full(sgug), full(sguw), full(sgub),
        ],
        out_specs=(
            pl.BlockSpec((ns, TM, nq), _q_tile_index(ctx_tiles, nt)),
            pl.BlockSpec((ns, TM, LANES), lambda i, j: (i, j, 0)),
            pl.BlockSpec((ns, TM, A_KV_HEADS * LANES), lambda i, j: (i, j, 0)),
            pl.BlockSpec((ns, TM, MIX_HALF), lambda i, j: (i, j, 0)),
        ),
        compiler_params=pltpu.CompilerParams(
            dimension_semantics=("arbitrary", "arbitrary"), vmem_limit_bytes=VMEM_LIMIT),
        name="even_in_proj",
    )(ctx, x, mod, g1, w_in, gq, gk, cos, sin, gmat64, sgug, sguw, sgub)


def _attn_body(q_ref, k_ref, v_ref, o_ref, n_keys, groups):
    for r0 in range(0, q_ref.shape[0], TM):
        _attn_rows(q_ref.at[r0:r0 + TM, :], k_ref, v_ref, o_ref.at[r0:r0 + TM, :], n_keys, groups)


def _attn_rows(q_ref, k_ref, v_ref, o_ref, n_keys, groups):
    tm = q_ref.shape[0]
    half = LANES // 2
    low = lax.broadcasted_iota(jnp.int32, (tm, LANES), 1) < half
    res = []
    for h, (qs, q_half, ks, vs) in enumerate(groups):
        q = q_ref[:, qs * LANES:(qs + 1) * LANES]
        if q_half is not None:
            q = jnp.where(low if q_half == 0 else jnp.logical_not(low), q.astype(F32), 0.0).astype(BF16)
        s = _dot_nt(q, k_ref[:n_keys, ks * LANES:(ks + 1) * LANES])
        p = jnp.exp2(s - jnp.max(s, axis=-1, keepdims=True))
        vp = vs // 2 * 2
        pv = _dot(p.astype(BF16), v_ref[:n_keys, vp * LANES:(vp + 2) * LANES])
        pv = pv[:, (vs - vp) * LANES:(vs - vp + 1) * LANES]
        sw = pltpu.roll(pv, half, 1)
        res.append(pv / sw if h % 2 == 0 else sw / pv)
    for pair in range(len(res) // 2):
        o_ref[:, pair * LANES:(pair + 1) * LANES] = jnp.where(
            low, res[2 * pair], res[2 * pair + 1]).astype(BF16)


def _conv_prepare(yp, yc, yn, has_prev, has_next, ybuf, ysh):
    tm = yc.shape[0]
    ybuf[0:HALO, :] = jnp.where(has_prev, yp, 0.0)
    ybuf[HALO:HALO + tm, :] = yc
    ybuf[HALO + tm:, :] = jnp.where(has_next, yn, 0.0)
    n_sh = ysh.shape[1]
    for s in range(1, SUBLANES):
        ysh[s - 1] = ybuf[s:s + n_sh, :]


def _conv_ln_silu_rows(r0, ybuf, ysh, cw_ref, cb_ref, lng_ref, lnb_ref):
    base = HALO - D_CONV // 2
    nch = ybuf.shape[1]
    acc = None
    for tap in range(D_CONV):
        a, s = divmod(base + tap, SUBLANES)
        lo = SUBLANES * a + r0
        win = ybuf[lo:lo + CONV_ROWS, :] if s == 0 else ysh[s - 1, lo:lo + CONV_ROWS, :]
        term = win.reshape(CONV_ROWS // SUBLANES, SUBLANES, nch) * cw_ref[tap]
        acc = term if acc is None else acc + term
    y = acc.reshape(CONV_ROWS, nch) + cb_ref[...]
    mu = jnp.mean(y, axis=-1, keepdims=True)
    yc = y - mu
    var = jnp.mean(yc * yc, axis=-1, keepdims=True)
    ln = yc * lax.rsqrt(var + EPS) * lng_ref[...] + lnb_ref[...]
    return (ln * (1.0 / (1.0 + jnp.exp(-ln)))).astype(BF16)


def _attn_kernel(q_ref, k_ref, v_ref, o_ref, *, groups):
    _attn_body(q_ref, k_ref, v_ref, o_ref, k_ref.shape[0], groups)


def _attn_call(q, k, v, seq, groups, name):
    b, t, nq = q.shape
    rows = ATT_TILES * TM
    return pl.pallas_call(
        functools.partial(_attn_kernel, groups=groups),
        out_shape=jax.ShapeDtypeStruct((b, seq, MIX_HALF), BF16),
        grid=(b, seq // rows),
        in_specs=[
            pl.BlockSpec((None, rows, nq), lambda i, j: (i, j, 0)),
            pl.BlockSpec((None, t, k.shape[-1]), lambda i, j: (i, 0, 0)),
            pl.BlockSpec((None, t, v.shape[-1]), lambda i, j: (i, 0, 0)),
        ],
        out_specs=pl.BlockSpec((None, rows, MIX_HALF), lambda i, j: (i, j, 0)),
        compiler_params=pltpu.CompilerParams(
            dimension_semantics=("arbitrary", "arbitrary"), vmem_limit_bytes=VMEM_LIMIT),
        name=name,
    )(q, k, v)


def _ctx_attn_call(q, k, v, seq, n_ctx, groups, name):
    b, t, nq = q.shape
    assert seq % n_ctx == 0
    return pl.pallas_call(
        functools.partial(_attn_kernel, groups=groups),
        out_shape=jax.ShapeDtypeStruct((b, n_ctx, MIX_HALF), BF16),
        grid=(b,),
        in_specs=[
            pl.BlockSpec((None, n_ctx, nq), lambda i: (i, seq // n_ctx, 0)),
            pl.BlockSpec((None, n_ctx, k.shape[-1]), lambda i: (i, 0, 0)),
            pl.BlockSpec((None, n_ctx, v.shape[-1]), lambda i: (i, 0, 0)),
        ],
        out_specs=pl.BlockSpec((None, n_ctx, MIX_HALF), lambda i: (i, 0, 0)),
        compiler_params=pltpu.CompilerParams(
            dimension_semantics=("arbitrary",), vmem_limit_bytes=VMEM_LIMIT),
        name=name,
    )(q, k, v)


def _odd_in_kernel(x_ref, mod_ref, g1_ref, w_ref, gcq_ref, gckv_ref, wuq_ref, wukv_ref,
                   cos_ref, sin_ref, q_ref, k_ref, v_ref, y_ref):
    for s in range(x_ref.shape[0]):
        _odd_in_tile(x_ref.at[s], mod_ref.at[s], g1_ref, w_ref, gcq_ref, gckv_ref, wuq_ref, wukv_ref,
                     cos_ref, sin_ref, q_ref.at[s], k_ref.at[s], v_ref.at[s], y_ref.at[s])


def _odd_in_tile(x_ref, mod_ref, g1_ref, w_ref, gcq_ref, gckv_ref, wuq_ref, wukv_ref,
                 cos_ref, sin_ref, q_ref, k_ref, v_ref, y_ref):
    x = x_ref[...]
    h = _modulate(x, g1_ref[...], mod_ref[0:1, :], mod_ref[1:2, :])
    p = _dot(h.astype(BF16), w_ref[...])
    cq = p[:, :C_Q_RANK]
    ckv = p[:, C_Q_RANK:C_Q_RANK + C_KV_RANK]
    o = C_Q_RANK + C_KV_RANK
    kr = p[:, o:o + LANES]
    z = p[:, o + LANES:]
    cos = cos_ref[...]
    sin = sin_ref[...]

    q = _dot((_rms(cq) * gcq_ref[...]).astype(BF16), wuq_ref[...])
    q_ref[...] = (_rope(q, cos, sin) * ((C_NOPE + C_ROPE) ** -0.5 * LOG2E)).astype(BF16)
    kv = _dot((_rms(ckv) * gckv_ref[...]).astype(BF16), wukv_ref[...])
    nk = C_HEADS * LANES
    krr = _rope(kr, cos, sin)
    k_ref[...] = (kv[:, :nk] + jnp.concatenate([krr] * C_HEADS, axis=-1)).astype(BF16)
    v = kv[:, nk:]
    high = lax.broadcasted_iota(jnp.int32, v.shape, 1) % LANES >= C_V
    v_ref[...] = jnp.where(high, 1.0, v).astype(BF16)
    a = z[:, :MIX_HALF]
    gt = z[:, MIX_HALF:]
    y_ref[...] = a * (1.0 / (1.0 + jnp.exp(-gt)))


def _odd_in_call(xs, mod, layer, g1, w_in, gcq, gckv, wuq, wukv, cos, sin, n_ctx_rows):
    b, t, d = xs.shape
    ns = IN_SAMPLES
    nt = t // TM
    nk = C_HEADS * LANES
    full = lambda a: pl.BlockSpec(a.shape, lambda i, j: (0,) * a.ndim)
    return pl.pallas_call(
        _odd_in_kernel,
        out_shape=(
            jax.ShapeDtypeStruct((b, t, nk), BF16),
            jax.ShapeDtypeStruct((b, t, nk), BF16),
            jax.ShapeDtypeStruct((b, t, nk), BF16),
            jax.ShapeDtypeStruct((b, t, MIX_HALF), F32),
        ),
        grid=(b // ns, nt),
        in_specs=[
            pl.BlockSpec((ns, TM, d), lambda i, j: (i, j, 0)),
            pl.BlockSpec((None, ns, N_MOD, d),
                         lambda i, j: (layer, jnp.where(j < n_ctx_rows // TM, b // ns, i), 0, 0)),
            full(g1),
            pl.BlockSpec(memory_space=pltpu.VMEM),
            full(gcq), full(gckv),
            pl.BlockSpec(memory_space=pltpu.VMEM),
            pl.BlockSpec(memory_space=pltpu.VMEM),
            pl.BlockSpec((TM, LANES), lambda i, j: (j, 0)),
            pl.BlockSpec((TM, LANES), lambda i, j: (j, 0)),
        ],
        out_specs=(
            pl.BlockSpec((ns, TM, nk), _q_tile_index(n_ctx_rows // TM, nt)),
            pl.BlockSpec((ns, TM, nk), lambda i, j: (i, j, 0)),
            pl.BlockSpec((ns, TM, nk), lambda i, j: (i, j, 0)),
            pl.BlockSpec((ns, TM, MIX_HALF), lambda i, j: (i, j, 0)),
        ),
        compiler_params=pltpu.CompilerParams(
            dimension_semantics=("arbitrary", "arbitrary"), vmem_limit_bytes=VMEM_LIMIT),
        name="odd_in_proj",
    )(xs, mod, g1, w_in, gcq, gckv, wuq, wukv, cos, sin)


def _mlp(h2, w1_ref, w2_ref):
    acc = None
    for c in range(0, FF_DIM, FF_CHUNK):
        hc = jnp.maximum(_dot(h2, w1_ref[:, c:c + FF_CHUNK]), 0.0)
        part = _dot((hc * hc).astype(BF16), w2_ref[c:c + FF_CHUNK, :])
        acc = part if acc is None else acc + part
    return acc


def _even_out_kernel(ctx_ref, x_ref, actx_ref, alat_ref, mb_ref, mod_ref, g2_ref, wo_ref, w1_ref,
                     w2_ref, o_ref, *, ctx_tiles):
    ns, tm = x_ref.shape[0], x_ref.shape[1]
    is_ctx = pl.program_id(1) < ctx_tiles
    stack = lambda parts: jnp.concatenate(parts, axis=0)
    attn = jnp.where(is_ctx, stack([actx_ref[s] for s in range(ns)]),
                     stack([alat_ref[s] for s in range(ns)]))
    a = (_dot(attn, wo_ref[:MIX_HALF, :])
         + _dot(stack([mb_ref[s] for s in range(ns)]), wo_ref[MIX_HALF:, :]))
    x1 = [jnp.where(is_ctx, ctx_ref[s], x_ref[s]) + mod_ref[s, 2:3, :] * a[s * tm:(s + 1) * tm]
          for s in range(ns)]
    h2 = stack([_modulate(x1[s], g2_ref[...], mod_ref[s, 3:4, :], mod_ref[s, 4:5, :])
                for s in range(ns)]).astype(BF16)
    acc = _mlp(h2, w1_ref, w2_ref)
    for s in range(ns):
        o_ref[s] = x1[s] + mod_ref[s, 5:6, :] * acc[s * tm:(s + 1) * tm]


def _conv_kernel(yp_ref, yc_ref, yn_ref, cw_ref, cb_ref, lng_ref, lnb_ref, mb_ref, ybuf, ysh, *,
                 tile0, ctx_tiles, last_tile):
    j = pl.program_id(1) + tile0
    has_prev = jnp.logical_and(j > 0, j != ctx_tiles)
    has_next = jnp.logical_and(j < last_tile, j != ctx_tiles - 1)
    ns, tm = yc_ref.shape[0], yc_ref.shape[1]
    for s in range(ns):
        _conv_prepare(yp_ref[s], yc_ref[s], yn_ref[s], has_prev, has_next, ybuf.at[s], ysh.at[s])
        for r0 in range(0, tm, CONV_ROWS):
            mb_ref[s, r0:r0 + CONV_ROWS, :] = _conv_ln_silu_rows(
                r0, ybuf.at[s], ysh.at[s], cw_ref, cb_ref, lng_ref, lnb_ref)


def _odd_out_kernel(x_ref, attn_ref, mb_ref, mod_ref, g2_ref, wo_ref, w1_ref, w2_ref, fg_ref, o_ref,
                    *, final_norm):
    ns, tm = x_ref.shape[0], x_ref.shape[1]
    stack = lambda parts: jnp.concatenate(parts, axis=0)
    a = (_dot(stack([attn_ref[s] for s in range(ns)]), wo_ref[:MIX_HALF, :])
         + _dot(stack([mb_ref[s] for s in range(ns)]), wo_ref[MIX_HALF:, :]))
    x1 = [x_ref[s] + mod_ref[s, 2:3, :] * a[s * tm:(s + 1) * tm] for s in range(ns)]
    h2 = stack([_modulate(x1[s], g2_ref[...], mod_ref[s, 3:4, :], mod_ref[s, 4:5, :])
                for s in range(ns)]).astype(BF16)
    acc = _mlp(h2, w1_ref, w2_ref)
    for s in range(ns):
        out = x1[s] + mod_ref[s, 5:6, :] * acc[s * tm:(s + 1) * tm]
        if final_norm:
            out = _rms(out) * fg_ref[...]
        o_ref[s] = out


def _out_specs(ns, b, d, layer, ctx_tiles, tile0):
    half_spec = pl.BlockSpec((ns, TM, MIX_HALF), lambda i, j: (i, j, 0))
    mod_spec = pl.BlockSpec((None, ns, N_MOD, d),
                            lambda i, j: (layer, jnp.where(j + tile0 < ctx_tiles, b // ns, i), 0, 0))
    out_spec = pl.BlockSpec((ns, TM, d), lambda i, j: (i, j, 0))
    params = pltpu.CompilerParams(
        dimension_semantics=("arbitrary", "arbitrary"), vmem_limit_bytes=VMEM_LIMIT)
    return half_spec, mod_spec, out_spec, params


def _even_out_call(ctx, x, attn_ctx, attn_lat, mixb, mod, layer, g2, wo, w1, w2):
    b, seq, d = x.shape
    ns = OUT_SAMPLES
    ctx_tiles = ctx.shape[1] // TM
    nt = ctx_tiles + seq // TM
    full = lambda a: pl.BlockSpec(a.shape, lambda i, j: (0,) * a.ndim)
    whole = pl.BlockSpec(memory_space=pltpu.VMEM)
    half_spec, mod_spec, out_spec, params = _out_specs(ns, b, d, layer, ctx_tiles, 0)
    return pl.pallas_call(
        functools.partial(_even_out_kernel, ctx_tiles=ctx_tiles),
        out_shape=jax.ShapeDtypeStruct((b, nt * TM, d), F32),
        grid=(b // ns, nt),
        in_specs=[pl.BlockSpec((ns, TM, d), lambda i, j: (i, jnp.minimum(j, ctx_tiles - 1), 0)),
                  pl.BlockSpec((ns, TM, d), lambda i, j: (i, jnp.maximum(j - ctx_tiles, 0), 0)),
                  pl.BlockSpec((ns, TM, MIX_HALF),
                               lambda i, j: (i, jnp.minimum(j, ctx_tiles - 1), 0)),
                  pl.BlockSpec((ns, TM, MIX_HALF),
                               lambda i, j: (i, jnp.maximum(j - ctx_tiles, 0), 0)),
                  half_spec, mod_spec, full(g2), whole, whole, whole],
        out_specs=out_spec, compiler_params=params, name="even_out_mlp",
    )(ctx, x, attn_ctx, attn_lat, mixb, mod, g2, wo, w1, w2)


def _odd_out_call(xs, attn, y, mod, layer, g2, wo, w1, w2, cw, cb, lng, lnb, fg, n_ctx_rows, tile0,
                  final_norm):
    b, t, d = xs.shape
    ns = OUT_SAMPLES
    nt = t // TM - tile0
    ctx_tiles = n_ctx_rows // TM
    full = lambda a: pl.BlockSpec(a.shape, lambda i, j: (0,) * a.ndim)
    whole = pl.BlockSpec(memory_space=pltpu.VMEM)
    half_spec, mod_spec, out_spec, params = _out_specs(ns, b, d, layer, ctx_tiles, tile0)
    hb = TM // HALO
    last_halo = t // HALO - 1
    mixb = pl.pallas_call(
        functools.partial(_conv_kernel, tile0=tile0, ctx_tiles=ctx_tiles, last_tile=t // TM - 1),
        out_shape=jax.ShapeDtypeStruct((b, nt * TM, MIX_HALF), BF16),
        grid=(b // ns, nt),
        in_specs=[pl.BlockSpec((ns, HALO, MIX_HALF),
                               lambda i, j: (i, jnp.maximum((j + tile0) * hb - 1, 0), 0)),
                  pl.BlockSpec((ns, TM, MIX_HALF), lambda i, j: (i, j + tile0, 0)),
                  pl.BlockSpec((ns, HALO, MIX_HALF),
                               lambda i, j: (i, jnp.minimum((j + tile0 + 1) * hb, last_halo), 0)),
                  full(cw), full(cb), full(lng), full(lnb)],
        out_specs=half_spec,
        scratch_shapes=[pltpu.VMEM((ns, TM + 2 * HALO, MIX_HALF), F32),
                        pltpu.VMEM((ns, SUBLANES - 1, TM + 2 * HALO - SUBLANES, MIX_HALF), F32)],
        compiler_params=params, name="odd_conv",
    )(y, y, y, cw, cb, lng, lnb)
    return pl.pallas_call(
        functools.partial(_odd_out_kernel, final_norm=final_norm),
        out_shape=jax.ShapeDtypeStruct((b, nt * TM, d), F32),
        grid=(b // ns, nt),
        in_specs=[pl.BlockSpec((ns, TM, d), lambda i, j: (i, j + tile0, 0)),
                  half_spec, half_spec, mod_spec, full(g2), whole, whole, whole, full(fg)],
        out_specs=out_spec,
        compiler_params=params, name="odd_out_mlp",
    )(xs, attn, mixb, mod, g2, wo, w1, w2, fg)


def _axial_angles(length, d_rot):
    rows = length // GRID_W
    row = np.repeat(np.arange(rows), GRID_W).astype(np.float64)
    col = np.tile(np.arange(GRID_W), rows).astype(np.float64)
    d_axis = d_rot // 2
    inv = ROPE_THETA ** (-np.arange(0, d_axis, 2, dtype=np.float64) / d_axis)
    return np.concatenate([row[:, None] * inv, col[:, None] * inv], axis=-1)


def _rope_tables(seq, n_ctx, d_rot, lane0, period):
    ang = _axial_angles(seq, d_rot)
    cos = np.repeat(np.cos(ang), 2, axis=-1)
    sin = np.repeat(np.sin(ang), 2, axis=-1) * np.tile(np.array([-1.0, 1.0]), d_rot // 2)
    cos_p = np.ones((seq, period))
    sin_p = np.zeros((seq, period))
    cos_p[:, lane0:lane0 + d_rot] = cos
    sin_p[:, lane0:lane0 + d_rot] = sin
    cos_p = np.tile(cos_p, (1, LANES // period))
    sin_p = np.tile(sin_p, (1, LANES // period))
    cos_t = np.concatenate([np.ones((n_ctx, LANES)), cos_p], axis=0)
    sin_t = np.concatenate([np.zeros((n_ctx, LANES)), sin_p], axis=0)
    return jnp.asarray(cos_t, F32), jnp.asarray(sin_t, F32)


def _even_weights(w_in, q_g, k_g, sgu_g, sgu_w, sgu_b):
    d = w_in.shape[0]
    ev_q = A_Q_HEADS * A_HEAD_DIM
    ev_kv = A_KV_HEADS * A_HEAD_DIM
    wq = w_in[:, :ev_q].reshape(d, A_Q_HEADS, A_HEAD_DIM)
    assert A_KV_HEADS == 2
    order = [h for p in range(A_GROUP) for h in (p, p + A_GROUP)]
    w = jnp.concatenate([wq[:, h] for h in order] + [w_in[:, ev_q:]], axis=1).astype(BF16)
    gq = jnp.tile(q_g, A_Q_HEADS)[None, :]
    gk = jnp.tile(k_g, A_KV_HEADS)[None, :]
    sgug = sgu_g.reshape(1, -1)
    sgub = jnp.repeat(sgu_b.T, B_GROUP_DIM, axis=1)
    return w, gq, gk, sgug, sgu_w.astype(BF16), sgub


def _odd_weights(w_in, w_uq, w_ukv):
    d = w_in.shape[0]
    c0 = C_Q_RANK + C_KV_RANK
    w_kr = w_in[:, c0:c0 + C_ROPE]
    kr_slab = jnp.zeros((d, LANES), w_in.dtype).at[:, C_NOPE:C_NOPE + C_ROPE].set(w_kr)
    w = jnp.concatenate([w_in[:, :c0], kr_slab, w_in[:, c0 + C_ROPE:]], axis=1).astype(BF16)
    uq = w_uq.reshape(C_Q_RANK, C_HEADS, C_NOPE + C_ROPE)
    uq = jnp.pad(uq, ((0, 0), (0, 0), (0, LANES - C_NOPE - C_ROPE))).reshape(C_Q_RANK, C_HEADS * LANES)
    ukv = w_ukv.reshape(C_KV_RANK, C_HEADS, C_NOPE + C_V)
    uk = jnp.pad(ukv[:, :, :C_NOPE], ((0, 0), (0, 0), (0, LANES - C_NOPE))).reshape(C_KV_RANK, -1)
    uv = jnp.pad(ukv[:, :, C_NOPE:], ((0, 0), (0, 0), (0, LANES - C_V))).reshape(C_KV_RANK, -1)
    return w, uq.astype(BF16), jnp.concatenate([uk, uv], axis=1).astype(BF16)


def _group_matrix(group_lanes):
    idx = np.arange(2 * LANES) // group_lanes
    return jnp.asarray((idx[:, None] == idx[None, :]).astype(np.float32) / A_HEAD_DIM, BF16)


def kernel(x, c, ctx, c_ctx, ada_w, ada_b, norm1_g, norm2_g, w_out, mlp_w1, mlp_w2, ev_w_in, ev_q_norm_g, ev_k_norm_g, ev_sgu_norm_g, ev_sgu_w, ev_sgu_b, od_w_in, od_q_norm_g, od_kv_norm_g, od_w_uq, od_w_ukv, od_conv_w, od_conv_b, od_ln_g, od_ln_b, final_g):
    b, seq, d = x.shape
    n_ctx = ctx.shape[1]
    depth = ada_w.shape[0]
    assert d == D_MODEL and seq % TM == 0 and n_ctx % TM == 0 and depth == 2
    ctx_tiles = n_ctx // TM

    n_rep = max(IN_SAMPLES, OUT_SAMPLES)
    assert b % IN_SAMPLES == 0 and b % OUT_SAMPLES == 0
    r = -(-(b + n_rep) // SUBLANES) * SUBLANES
    cvec = jnp.concatenate([c] + [c_ctx[None, :]] * n_rep
                           + [jnp.zeros((r - b - n_rep, d), F32)], axis=0)
    mod = _ada_call(cvec, ada_w, ada_b).reshape(depth, r, N_MOD, d)
    fg = final_g[None, :]

    wo = w_out.astype(BF16)
    w1 = mlp_w1.astype(BF16)
    w2 = mlp_w2.astype(BF16)

    cos_e, sin_e = _rope_tables(seq, n_ctx, A_HEAD_DIM, 0, A_HEAD_DIM)
    w_e, gq, gk, sgug, sguw, sgub = _even_weights(ev_w_in[0], ev_q_norm_g[0], ev_k_norm_g[0],
                                                  ev_sgu_norm_g[0], ev_sgu_w[0], ev_sgu_b[0])
    q, k, v, mixb = _even_in_call(ctx, x, mod, 0, norm1_g[0][None, :], w_e, gq, gk, cos_e, sin_e,
                                  _group_matrix(A_HEAD_DIM), sgug, sguw, sgub)
    groups = tuple((h % A_GROUP, h // A_GROUP, 0, h // A_GROUP) for h in range(A_Q_HEADS))
    attn_lat = _attn_call(q, k, v, seq, groups, "even_attention")
    attn_ctx = _ctx_attn_call(q, k, v, seq, n_ctx, groups, "even_ctx_attention")
    xs = _even_out_call(ctx, x, attn_ctx, attn_lat, mixb, mod, 0, norm2_g[0][None, :], wo[0], w1[0],
                        w2[0])

    cos_o, sin_o = _rope_tables(seq, n_ctx, C_ROPE, C_NOPE, LANES)
    w_o, uq, ukv = _odd_weights(od_w_in[0], od_w_uq[0], od_w_ukv[0])
    q, k, v, y = _odd_in_call(xs, mod, 1, norm1_g[1][None, :], w_o, od_q_norm_g[0][None, :],
                              od_kv_norm_g[0][None, :], uq, ukv, cos_o, sin_o, n_ctx)
    groups = tuple((h, None, h, h) for h in range(C_HEADS))
    cw = jnp.broadcast_to(od_conv_w[0][:, None, :], (D_CONV, SUBLANES, MIX_HALF))
    attn = _attn_call(q, k, v, seq, groups, "odd_attention")
    return _odd_out_call(xs, attn, y, mod, 1, norm2_g[1][None, :], wo[1], w1[1], w2[1], cw,
                         od_conv_b[0][None, :], od_ln_g[0][None, :], od_ln_b[0][None, :], fg, n_ctx,
                         ctx_tiles, True)
```
